```python
import jax, jax.numpy as jnp
from jax import lax
import numpy as np

D_MODEL = 1024
BATCH = 2
SEQ = 16384
DEPTH = 2

HEAD_DIM = 64
ATT_SLOTS = 8
DIL_PATTERNS = ((128, 1), (512, 4), (2048, 16))
N_DIL = 3
ATT_SUBHEADS = ATT_SLOTS * N_DIL
ATT_BLOCK = 128
ATT_Q = ATT_SUBHEADS * HEAD_DIM
ATT_KV = ATT_SLOTS * HEAD_DIM
ALIBI_MAX_EXP = 8.0
CONV_CH = 512
CONV_K = 31
MLSTM_HEADS = 4
MLSTM_DH = 128
MLSTM_W = MLSTM_HEADS * MLSTM_DH
MLSTM_CONV_K = 4
MLSTM_CHUNK = 128
N_BRANCH = 3
N_GROUPS_MOE = 4
EXPERTS_PER_GROUP = 8
N_EXPERTS = N_GROUPS_MOE * EXPERTS_PER_GROUP
TOP_K = 2
D_EXPERT = 256
MOE_BLOCK = 128
PLE_DIM = 256
EPS = 1e-6

SPLITS = (ATT_Q, ATT_KV, ATT_KV, 2 * CONV_CH, MLSTM_W, MLSTM_W, MLSTM_W, MLSTM_W,
          MLSTM_HEADS, MLSTM_HEADS, N_BRANCH * D_MODEL)
D_IN = sum(SPLITS)
SPLIT_IDX = tuple(int(c) for c in np.cumsum(SPLITS)[:-1])

kernel_name = "hybrid_dilated_conv_mlstm_hmoe"


def rmsnorm(x, g):
    xf = x.astype(jnp.float32)
    r = lax.rsqrt(jnp.mean(xf * xf, axis=-1, keepdims=True) + EPS)
    return (xf * r).astype(x.dtype) * g


def layernorm(x, g, b):
    xf = x.astype(jnp.float32)
    mu = jnp.mean(xf, axis=-1, keepdims=True)
    var = jnp.mean(jnp.square(xf - mu), axis=-1, keepdims=True)
    return ((xf - mu) * lax.rsqrt(var + EPS)).astype(x.dtype) * g + b


def causal_dwconv(x, w, b):
    k, c = w.shape
    y = lax.conv_general_dilated(x, w[:, None, :], window_strides=(1,), padding=((k - 1, 0),),
                                 dimension_numbers=("NWC", "WIO", "NWC"), feature_group_count=c)
    return y + b


def alibi_slopes():
    j = jnp.arange(1, ATT_SUBHEADS + 1, dtype=jnp.float32)
    return (2.0 ** (-ALIBI_MAX_EXP * j / ATT_SUBHEADS)).reshape(N_DIL, ATT_SLOTS)


def _window_attend(q, k, v, slopes, dilation, steps):
    n, l, h, hd = q.shape
    nb = l // ATT_BLOCK
    qb = q.reshape(n, nb, ATT_BLOCK, h, hd)
    kb = k.reshape(n, nb, ATT_BLOCK, h, hd)
    vb = v.reshape(n, nb, ATT_BLOCK, h, hd)
    prev = ((0, 0), (1, 0), (0, 0), (0, 0), (0, 0))
    k2 = jnp.concatenate([jnp.pad(kb[:, :-1], prev), kb], axis=2)
    v2 = jnp.concatenate([jnp.pad(vb[:, :-1], prev), vb], axis=2)
    qi = jnp.arange(ATT_BLOCK)[:, None] + ATT_BLOCK
    ki = jnp.arange(2 * ATT_BLOCK)[None, :]
    dist = qi - ki
    kabs = jnp.arange(nb)[:, None, None] * ATT_BLOCK + ki[None] - ATT_BLOCK
    valid = (dist >= 0) & (dist <= steps) & (kabs >= 0)
    bias = -(slopes[:, None, None] * (dilation * dist).astype(jnp.float32))
    s = jnp.einsum("nbqhd,nbkhd->nbhqk", qb, k2).astype(jnp.float32) + bias
    s = jnp.where(valid[None, :, None], s, -jnp.inf)
    m = jnp.max(s, axis=-1, keepdims=True)
    pr = jnp.exp(s - m)
    den = jnp.sum(pr, axis=-1, keepdims=True)
    o = jnp.einsum("nbhqk,nbkhd->nbqhd", pr / den, v2.astype(jnp.float32))
    lse = (m + jnp.log(den))[..., 0]
    return o.reshape(n, l, h, hd), lse.transpose(0, 1, 3, 2).reshape(n, l, h)


def dilated_attention(q, k, v):
    b, s = q.shape[0], q.shape[1]
    slopes = alibi_slopes()
    q = q * HEAD_DIM ** -0.5
    outs, lses = [], []
    for g, (window, dil) in enumerate(DIL_PATTERNS):
        l = s // dil
        lp = -(-l // ATT_BLOCK) * ATT_BLOCK

        def to_sub(t):
            t = t.reshape(b, l, dil, ATT_SLOTS, HEAD_DIM).transpose(0, 2, 1, 3, 4)
            t = t.reshape(b * dil, l, ATT_SLOTS, HEAD_DIM)
            return jnp.pad(t, ((0, 0), (0, lp - l), (0, 0), (0, 0)))

        o, lse = _window_attend(to_sub(q[:, :, g]), to_sub(k), to_sub(v), slopes[g], dil, window // dil)
        o = o[:, :l].reshape(b, dil, l, ATT_SLOTS, HEAD_DIM).transpose(0, 2, 1, 3, 4)
        lse = lse[:, :l].reshape(b, dil, l, ATT_SLOTS).transpose(0, 2, 1, 3)
        outs.append(o.reshape(b, s, ATT_SLOTS, HEAD_DIM))
        lses.append(lse.reshape(b, s, ATT_SLOTS))
    wts = jax.nn.softmax(jnp.stack(lses, axis=0), axis=0)
    o = jnp.einsum("gbsh,gbshd->bshd", wts, jnp.stack(outs, axis=0))
    return o.reshape(b, s, ATT_SLOTS * HEAD_DIM).astype(k.dtype)


def conformer_conv(u, dw_w, dw_b, ln_g, ln_b):
    a, gt = jnp.split(u, 2, axis=-1)
    y = a * jax.nn.sigmoid(gt)
    y = causal_dwconv(y, dw_w, dw_b)
    y = layernorm(y, ln_g, ln_b)
    return jax.nn.silu(y)


def mlstm(q, k, v, i_pre, f_pre):
    b, s, h, dh = q.shape
    lc = MLSTM_CHUNK
    nc = s // lc
    f32 = jnp.float32

    def chunks(t):
        return t.astype(f32).reshape(b, nc, lc, h, -1).transpose(0, 3, 1, 2, 4)

    def gchunks(t):
        return t.astype(f32).reshape(b, nc, lc, h).transpose(0, 3, 1, 2)

    qc, kc, vc = chunks(q), chunks(k) * dh ** -0.5, chunks(v)
    ig = gchunks(i_pre)
    lf = jax.nn.log_sigmoid(gchunks(f_pre))
    bcum = jnp.cumsum(lf, axis=-1)
    gtot = bcum[..., -1]
    causal = jnp.tril(jnp.ones((lc, lc), dtype=bool))
    dmat = bcum[..., :, None] - bcum[..., None, :] + ig[..., None, :]
    dmat = jnp.where(causal, dmat, -jnp.inf)
    wst = gtot[..., None] - bcum + ig
    m_loc = jnp.max(wst, axis=-1)
    e = jnp.exp(wst - m_loc[..., None])
    c_loc = jnp.einsum("bhcl,bhclv,bhclk->bhcvk", e, vc, kc)
    n_loc = jnp.einsum("bhcl,bhclk->bhck", e, kc)

    def step(carry, inp):
        c_prev, n_prev, m_prev = carry
        cl, nl, ml, gc = inp
        m_new = jnp.maximum(gc + m_prev, ml)
        a = jnp.exp(gc + m_prev - m_new)
        bb = jnp.exp(ml - m_new)
        c_new = a[..., None, None] * c_prev + bb[..., None, None] * cl
        n_new = a[..., None] * n_prev + bb[..., None] * nl
        return (c_new, n_new, m_new), (c_prev, n_prev, m_prev)

    init = (jnp.zeros((b, h, dh, dh), f32), jnp.zeros((b, h, dh), f32), jnp.zeros((b, h), f32))
    xs = (jnp.moveaxis(c_loc, 2, 0), jnp.moveaxis(n_loc, 2, 0), jnp.moveaxis(m_loc, 2, 0), jnp.moveaxis(gtot, 2, 0))
    _, (c_st, n_st, m_st) = lax.scan(step, init, xs)
    c_st = jnp.moveaxis(c_st, 0, 2)
    n_st = jnp.moveaxis(n_st, 0, 2)
    m_st = jnp.moveaxis(m_st, 0, 2)
    a_inter = bcum + m_st[..., None]
    m_t = jnp.maximum(a_inter, jnp.max(dmat, axis=-1))
    w_intra = jnp.exp(dmat - m_t[..., None])
    sc = jnp.einsum("bhcqd,bhckd->bhcqk", qc, kc) * w_intra
    s_inter = jnp.exp(a_inter - m_t)
    num = jnp.einsum("bhcqk,bhckv->bhcqv", sc, vc) + s_inter[..., None] * jnp.einsum("bhcvk,bhcqk->bhcqv", c_st, qc)
    den = jnp.sum(sc, axis=-1) + s_inter * jnp.einsum("bhck,bhcqk->bhcq", n_st, qc)
    hout = num / jnp.maximum(jnp.abs(den), jnp.exp(-m_t))[..., None]
    return hout.transpose(0, 2, 3, 1, 4).reshape(b, s, h * dh).astype(q.dtype)


def expert_dispatch(xf, eid, wts, w_gate, w_up, w_down):
    t, d = xf.shape
    a = t * TOP_K
    e_flat = eid.reshape(a)
    tok = jnp.arange(a, dtype=jnp.int32) // TOP_K
    order = jnp.argsort(e_flat)
    e_sorted = e_flat[order]
    counts = jnp.zeros((N_EXPERTS,), jnp.int32).at[e_flat].add(1)
    padded = (counts + MOE_BLOCK - 1) // MOE_BLOCK * MOE_BLOCK
    start = jnp.cumsum(counts) - counts
    ends_p = jnp.cumsum(padded)
    pstart = ends_p - padded
    dest = pstart[e_sorted] + (jnp.arange(a, dtype=jnp.int32) - start[e_sorted])
    p_len = a + N_EXPERTS * MOE_BLOCK
    nb = p_len // MOE_BLOCK
    buf_tok = jnp.full((p_len,), t, jnp.int32).at[dest].set(tok[order])
    buf_w = jnp.zeros((p_len,), xf.dtype).at[dest].set(wts.reshape(a)[order].astype(xf.dtype))
    blk_e = jnp.minimum(jnp.searchsorted(ends_p, jnp.arange(nb, dtype=jnp.int32) * MOE_BLOCK, side="right"), N_EXPERTS - 1)
    x_pad = jnp.concatenate([xf, jnp.zeros((1, d), xf.dtype)], axis=0)

    def run_block(args):
        idx, e, w = args
        xb = x_pad[idx]
        hb = jax.nn.silu(xb @ w_gate[e]) * (xb @ w_up[e])
        return (hb @ w_down[e]) * w[:, None]

    yb = lax.map(run_block, (buf_tok.reshape(nb, MOE_BLOCK), blk_e, buf_w.reshape(nb, MOE_BLOCK)))
    out = jnp.zeros((t + 1, d), yb.dtype).at[buf_tok].add(yb.reshape(p_len, d))
    return out[:t]


def hier_moe(h, w_rg, b_rg, w_re, b_re, w_gate, w_up, w_down):
    b, s, d = h.shape
    t = b * s
    hf = h.reshape(t, d)
    rows = jnp.arange(t)
    lg = (hf @ w_rg).astype(jnp.float32) + b_rg
    pg = jax.nn.softmax(lg, axis=-1)
    _, g_sel = lax.top_k(lg, 1)
    g_sel = g_sel[:, 0]
    p_g = pg[rows, g_sel]
    le = ((hf @ w_re).astype(jnp.float32) + b_re).reshape(t, N_GROUPS_MOE, EXPERTS_PER_GROUP)
    le_sel = le[rows, g_sel]
    top_v, top_i = lax.top_k(le_sel, TOP_K)
    wts = p_g[:, None] * jax.nn.softmax(top_v, axis=-1)
    eid = g_sel[:, None] * EXPERTS_PER_GROUP + top_i
    return expert_dispatch(hf, eid, wts, w_gate, w_up, w_down).reshape(b, s, d)


def setup_inputs(seed: int = 0) -> dict:
    key = jax.random.key(seed)
    ks = jax.random.split(key, 28)
    f32 = jnp.float32
    L, D = DEPTH, D_MODEL

    def nrm(k, shape, scale):
        return jax.random.normal(k, shape, f32) * scale

    def gain(k, shape):
        return 1.0 + 0.05 * jax.random.normal(k, shape, f32)

    return {
        "x": nrm(ks[0], (BATCH, SEQ, D), 1.0),
        "p": nrm(ks[1], (DEPTH, BATCH, SEQ, PLE_DIM), 1.0),
        "g_mix": gain(ks[2], (L, D)),
        "w_in": nrm(ks[3], (L, D, D_IN), D ** -0.5),
        "conv_w": nrm(ks[4], (L, CONV_K, CONV_CH), CONV_K ** -0.5),
        "conv_b": nrm(ks[5], (L, CONV_CH), 0.02),
        "conv_ln_g": gain(ks[6], (L, CONV_CH)),
        "conv_ln_b": nrm(ks[7], (L, CONV_CH), 0.02),
        "mlstm_conv_w": nrm(ks[8], (L, MLSTM_CONV_K, 2 * MLSTM_W), MLSTM_CONV_K ** -0.5),
        "mlstm_conv_b": nrm(ks[9], (L, 2 * MLSTM_W), 0.02),
        "mlstm_i_bias": nrm(ks[10], (L, MLSTM_HEADS), 0.1),
        "mlstm_f_bias": jnp.linspace(3.0, 6.0, MLSTM_HEADS, dtype=f32) + nrm(ks[11], (L, MLSTM_HEADS), 0.1),
        "w_br_att": nrm(ks[12], (L, ATT_KV, D), ATT_KV ** -0.5),
        "w_br_conv": nrm(ks[13], (L, CONV_CH, D), CONV_CH ** -0.5),
        "w_br_mlstm": nrm(ks[14], (L, MLSTM_W, D), MLSTM_W ** -0.5),
        "w_out": nrm(ks[15], (L, D, D), D ** -0.5),
        "g_ffn": gain(ks[16], (L, D)),
        "w_router_g": nrm(ks[17], (L, D, N_GROUPS_MOE), D ** -0.5),
        "b_router_g": nrm(ks[18], (L, N_GROUPS_MOE), 0.01),
        "w_router_e": nrm(ks[19], (L, D, N_EXPERTS), D ** -0.5),
        "b_router_e": nrm(ks[20], (L, N_EXPERTS), 0.01),
        "w_exp_gate": nrm(ks[21], (L, N_EXPERTS, D, D_EXPERT), D ** -0.5),
        "w_exp_up": nrm(ks[22], (L, N_EXPERTS, D, D_EXPERT), D ** -0.5),
        "w_exp_down": nrm(ks[23], (L, N_EXPERTS, D_EXPERT, D), D_EXPERT ** -0.5),
        "g_ple": gain(ks[24], (L, D)),
        "w_ple_gate": nrm(ks[25], (L, D, D), D ** -0.5),
        "w_ple_proj": nrm(ks[26], (L, PLE_DIM, D), PLE_DIM ** -0.5),
        "g_final": gain(ks[27], (D,)),
    }


def reference(x, p, g_mix, w_in, conv_w, conv_b, conv_ln_g, conv_ln_b, mlstm_conv_w, mlstm_conv_b,
              mlstm_i_bias, mlstm_f_bias, w_br_att, w_br_conv, w_br_mlstm, w_out, g_ffn,
              w_router_g, b_router_g, w_router_e, b_router_e, w_exp_gate, w_exp_up, w_exp_down,
              g_ple, w_ple_gate, w_ple_proj, g_final):
    b, s, d = x.shape
    for i in range(DEPTH):
        h = rmsnorm(x, g_mix[i])
        proj = h @ w_in[i]
        (q_a, k_a, v_a, u_conv, q_m, k_m, v_m, o_m, i_m, f_m, gate_pre) = jnp.split(proj, SPLIT_IDX, axis=-1)
        y_att = dilated_attention(q_a.reshape(b, s, N_DIL, ATT_SLOTS, HEAD_DIM),
                                  k_a.reshape(b, s, ATT_SLOTS, HEAD_DIM),
                                  v_a.reshape(b, s, ATT_SLOTS, HEAD_DIM))
        y_conv = conformer_conv(u_conv, conv_w[i], conv_b[i], conv_ln_g[i], conv_ln_b[i])
        qk = jax.nn.silu(causal_dwconv(jnp.concatenate([q_m, k_m], axis=-1), mlstm_conv_w[i], mlstm_conv_b[i]))
        q_c, k_c = jnp.split(qk, 2, axis=-1)
        h_m = mlstm(q_c.reshape(b, s, MLSTM_HEADS, MLSTM_DH), k_c.reshape(b, s, MLSTM_HEADS, MLSTM_DH),
                    v_m.reshape(b, s, MLSTM_HEADS, MLSTM_DH), i_m + mlstm_i_bias[i], f_m + mlstm_f_bias[i])
        y_m = jax.nn.sigmoid(o_m) * h_m
        gates = jax.nn.sigmoid(gate_pre).reshape(b, s, N_BRANCH, d)
        merged = (gates[..., 0, :] * (y_att @ w_br_att[i])
                  + gates[..., 1, :] * (y_conv @ w_br_conv[i])
                  + gates[..., 2, :] * (y_m @ w_br_mlstm[i]))
        x = x + merged @ w_out[i]
        h2 = rmsnorm(x, g_ffn[i])
        x = x + hier_moe(h2, w_router_g[i], b_router_g[i], w_router_e[i], b_router_e[i],
                         w_exp_gate[i], w_exp_up[i], w_exp_down[i])
        hp = rmsnorm(x, g_ple[i])
        x = x + jax.nn.sigmoid(hp @ w_ple_gate[i]) * (p[i] @ w_ple_proj[i])
    return rmsnorm(x, g_final)
```

```python
import functools

import numpy as np
import jax
import jax.numpy as jnp
from jax import lax
from jax.experimental import pallas as pl
from jax.experimental.pallas import tpu as pltpu

F32 = jnp.float32
BF16 = jnp.bfloat16
I32 = jnp.int32
HIGHEST = lax.Precision.HIGHEST

D_MODEL = 1024
HEAD_DIM = 64
ATT_SLOTS = 8
DILATIONS = (1, 4, 16)
N_DIL = 3
ATT_BLOCK = 128
ATT_Q = ATT_SLOTS * N_DIL * HEAD_DIM
ATT_KV = ATT_SLOTS * HEAD_DIM
ALIBI_MAX_EXP = 8.0
CONV_CH = 512
CONV_K = 31
MLSTM_HEADS = 4
MLSTM_DH = 128
MLSTM_W = MLSTM_HEADS * MLSTM_DH
MLSTM_CONV_K = 4
MLSTM_CHUNK = 128
N_GROUPS_MOE = 4
EXPERTS_PER_GROUP = 8
N_EXPERTS = N_GROUPS_MOE * EXPERTS_PER_GROUP
D_EXPERT = 256
MOE_BLOCK = 128
PLE_DIM = 256
EPS = 1e-6
NEG = -1e30

LANES = 128
SUBLANES = 8
ROW_TILES = D_MODEL // LANES
VMEM_LIMIT = 56 * 1024 * 1024

OFF_GATE = 0
OFF_QA = 3 * D_MODEL
OFF_KA = OFF_QA + ATT_Q
OFF_VA = OFF_KA + ATT_KV
OFF_CA = OFF_VA + ATT_KV
OFF_CG = OFF_CA + CONV_CH
OFF_QM = OFF_CG + CONV_CH
OFF_KM = OFF_QM + MLSTM_W
OFF_VM = OFF_KM + MLSTM_W
OFF_OM = OFF_VM + MLSTM_W
OFF_IF = OFF_OM + MLSTM_W
PROJ_N = OFF_IF + LANES
PROJ_TN = PROJ_N // 3
ORIG_SMALL = ATT_Q + 2 * ATT_KV + 2 * CONV_CH + 4 * MLSTM_W

ATT_TILE = ATT_BLOCK * DILATIONS[-1]


def _cparams(sem, vmem=VMEM_LIMIT):
    return pltpu.CompilerParams(dimension_semantics=sem, vmem_limit_bytes=vmem)


def _rms(x, g):
    r = lax.rsqrt(jnp.mean(x * x, axis=-1, keepdims=True) + EPS)
    return (x * r) * g


def _sigmoid(x):
    return 1.0 / (1.0 + jnp.exp(-x))


def _log_sigmoid(x):
    return jnp.minimum(x, 0.0) - jnp.log(1.0 + jnp.exp(-jnp.abs(x)))


def _inproj_kernel(x_ref, g_ref, w_ref, o_ref, h_scr):
    @pl.when(pl.program_id(1) == 0)
    def _():
        h_scr[...] = _rms(x_ref[...], g_ref[...]).astype(BF16)

    o_ref[...] = jnp.dot(h_scr[...], w_ref[...], preferred_element_type=F32)


def _inproj(x2, g, w, tm=512):
    t = x2.shape[0]
    return pl.pallas_call(
        _inproj_kernel,
        grid=(t // tm, PROJ_N // PROJ_TN),
        in_specs=[pl.BlockSpec((tm, D_MODEL), lambda i, j: (i, 0)),
                  pl.BlockSpec((1, D_MODEL), lambda i, j: (0, 0)),
                  pl.BlockSpec((D_MODEL, PROJ_TN), lambda i, j: (0, j))],
        out_specs=pl.BlockSpec((tm, PROJ_TN), lambda i, j: (i, j)),
        out_shape=jax.ShapeDtypeStruct((t, PROJ_N), F32),
        scratch_shapes=[pltpu.VMEM((tm, D_MODEL), BF16)],
        compiler_params=_cparams(("arbitrary", "arbitrary")),
    )(x2, g, w)


def _att_bias_table():
    j = np.arange(1, ATT_SLOTS * N_DIL + 1, dtype=np.float64)
    slopes = (2.0 ** (-ALIBI_MAX_EXP * j / (ATT_SLOTS * N_DIL))).reshape(N_DIL, ATT_SLOTS)
    qi = np.arange(ATT_BLOCK)[:, None] + ATT_BLOCK
    ki = np.arange(2 * ATT_BLOCK)[None, :]
    dist = qi - ki
    valid = (dist >= 0) & (dist <= ATT_BLOCK)
    tab = np.zeros((ATT_SLOTS // 2, N_DIL * 4, ATT_BLOCK, 2 * ATT_BLOCK), np.float32)
    for hp in range(ATT_SLOTS // 2):
        for g, dil in enumerate(DILATIONS):
            for h in range(2):
                slope = np.float32(slopes[g, 2 * hp + h])
                bias = -(slope * (dil * dist).astype(np.float32))
                tab[hp, g * 4 + h * 2 + 0] = np.where(valid, bias, NEG)
                tab[hp, g * 4 + h * 2 + 1] = np.where(valid & (ki >= ATT_BLOCK), bias, NEG)
    return tab


def _att_kernel(q0_ref, q1_ref, q2_ref, kc_ref, kp_ref, vc_ref, vp_ref, bias_ref, o_ref,
                qs, ks0, ks1, ks2, vs0, vs1, vs2, og, lg, onat, lnat):
    i = pl.program_id(2)
    q_refs = (q0_ref, q1_ref, q2_ref)
    k_scr = (ks0, ks1, ks2)
    v_scr = (vs0, vs1, vs2)
    scale = HEAD_DIM ** -0.5

    for g, dil in enumerate(DILATIONS):
        lt = ATT_TILE // dil
        cs = ATT_BLOCK + lt
        for r in range(dil):
            cur = pl.ds(r, lt, stride=dil) if dil > 1 else pl.ds(0, lt)
            prv = (pl.ds(ATT_TILE - ATT_BLOCK * dil + r, ATT_BLOCK, stride=dil) if dil > 1
                   else pl.ds(ATT_TILE - ATT_BLOCK, ATT_BLOCK))
            qs[g, r * lt:(r + 1) * lt, :] = (q_refs[g][cur, :] * scale).astype(BF16)
            k_scr[g][r * cs + ATT_BLOCK:(r + 1) * cs, :] = kc_ref[cur, :].astype(BF16)
            k_scr[g][r * cs:r * cs + ATT_BLOCK, :] = kp_ref[prv, :].astype(BF16)
            v_scr[g][r * cs + ATT_BLOCK:(r + 1) * cs, :] = vc_ref[cur, :].astype(BF16)
            v_scr[g][r * cs:r * cs + ATT_BLOCK, :] = vp_ref[prv, :].astype(BF16)

    lane = lax.broadcasted_iota(I32, (ATT_BLOCK, LANES), 1)
    head0 = lane < HEAD_DIM

    for g, dil in enumerate(DILATIONS):
        nj = ATT_TILE // dil // ATT_BLOCK

        def block(t, carry, g=g, nj=nj):
            r = t // nj
            j = t - r * nj
            qoff = pl.multiple_of(t * ATT_BLOCK, ATT_BLOCK)
            koff = pl.multiple_of((r * (nj + 1) + j) * ATT_BLOCK, ATT_BLOCK)
            first = jnp.logical_and(i == 0, j == 0).astype(I32)
            qb = qs[g, pl.ds(qoff, ATT_BLOCK), :]
            kb = k_scr[g][pl.ds(koff, 2 * ATT_BLOCK), :]
            vb = v_scr[g][pl.ds(koff, 2 * ATT_BLOCK), :]
            outs, lses = [], []
            for h in range(2):
                qh = jnp.where(head0 if h == 0 else jnp.logical_not(head0), qb, jnp.zeros_like(qb))
                s = lax.dot_general(qh, kb, (((1,), (1,)), ((), ())), preferred_element_type=F32)
                s = s + bias_ref[g * 4 + h * 2 + first]
                m = jnp.max(s, axis=-1, keepdims=True)
                p = jnp.exp(s - m)
                den = jnp.sum(p, axis=-1, keepdims=True)
                o = jnp.dot(p.astype(BF16), vb, preferred_element_type=F32) / den
                outs.append(o)
                lses.append(jnp.broadcast_to(m + jnp.log(den), (ATT_BLOCK, LANES)))
            og[g, pl.ds(qoff, ATT_BLOCK), :] = jnp.where(head0, outs[0], outs[1])
            lg[g, pl.ds(qoff, ATT_BLOCK), :] = jnp.where(head0, lses[0], lses[1])
            return carry

        lax.fori_loop(0, ATT_TILE // ATT_BLOCK, block, 0)

    for gi, dil in enumerate(DILATIONS[1:]):
        lt = ATT_TILE // dil
        for r in range(dil):
            onat[gi, pl.ds(r, lt, stride=dil), :] = og[gi + 1, r * lt:(r + 1) * lt, :]
            lnat[gi, pl.ds(r, lt, stride=dil), :] = lg[gi + 1, r * lt:(r + 1) * lt, :]

    rc = 256

    def merge(c, carry):
        rows = pl.ds(pl.multiple_of(c * rc, rc), rc)
        l0, l1, l2 = lg[0, rows, :], lnat[0, rows, :], lnat[1, rows, :]
        mx = jnp.maximum(jnp.maximum(l0, l1), l2)
        w0, w1, w2 = jnp.exp(l0 - mx), jnp.exp(l1 - mx), jnp.exp(l2 - mx)
        tot = w0 + w1 + w2
        o = (w0 / tot) * og[0, rows, :] + (w1 / tot) * onat[0, rows, :] + (w2 / tot) * onat[1, rows, :]
        o_ref[rows, :] = o.astype(o_ref.dtype)
        return carry

    lax.fori_loop(0, ATT_TILE // rc, merge, 0)


def _attention(proj, bias_tab, batch, seq):
    t = proj.shape[0]
    nt = seq // ATT_TILE
    cb = lambda off: off // LANES

    def cur(col):
        return pl.BlockSpec((ATT_TILE, LANES), lambda hp, b, i, col=col: (b * nt + i, col + hp))

    def prev(col):
        return pl.BlockSpec((ATT_TILE, LANES),
                            lambda hp, b, i, col=col: (b * nt + jnp.maximum(i - 1, 0), col + hp))

    npairs = ATT_SLOTS // 2
    kv_rows = [ATT_TILE + ATT_BLOCK * d for d in DILATIONS]
    return pl.pallas_call(
        _att_kernel,
        grid=(npairs, batch, nt),
        in_specs=[cur(cb(OFF_QA)), cur(cb(OFF_QA) + npairs), cur(cb(OFF_QA) + 2 * npairs),
                  cur(cb(OFF_KA)), prev(cb(OFF_KA)), cur(cb(OFF_VA)), prev(cb(OFF_VA)),
                  pl.BlockSpec((None, N_DIL * 4, ATT_BLOCK, 2 * ATT_BLOCK), lambda hp, b, i: (hp, 0, 0, 0))],
        out_specs=pl.BlockSpec((ATT_TILE, LANES), lambda hp, b, i: (b * nt + i, hp)),
        out_shape=jax.ShapeDtypeStruct((t, ATT_KV), BF16),
        scratch_shapes=([pltpu.VMEM((N_DIL, ATT_TILE, LANES), BF16)]
                        + [pltpu.VMEM((n, LANES), BF16) for n in kv_rows]
                        + [pltpu.VMEM((n, LANES), BF16) for n in kv_rows]
                        + [pltpu.VMEM((N_DIL, ATT_TILE, LANES), F32)] * 2
                        + [pltpu.VMEM((N_DIL - 1, ATT_TILE, LANES), F32)] * 2),
        compiler_params=_cparams(("arbitrary", "arbitrary", "arbitrary")),
    )(proj, proj, proj, proj, proj, proj, proj, bias_tab)


CONV_TILE = 256
CONV_HALO = 32
CONV_ROWS = 32


def _conv_kernel(a_ref, gt_ref, ha_ref, hg_ref, w_ref, b_ref, lng_ref, lnb_ref, o_ref, ybuf):
    i = pl.program_id(1)
    halo = ha_ref[...] * _sigmoid(hg_ref[...])
    ybuf[0:CONV_HALO, :] = jnp.where(i > 0, halo, jnp.zeros_like(halo))
    ybuf[CONV_HALO:, :] = a_ref[...] * _sigmoid(gt_ref[...])
    first = CONV_HALO - (CONV_K - 1)
    for c in range(CONV_TILE // CONV_ROWS):
        acc = jnp.broadcast_to(b_ref[...], (CONV_ROWS, CONV_CH))
        for k in range(CONV_K):
            lo = c * CONV_ROWS + first + k
            acc = acc + w_ref[k:k + 1, :] * ybuf[lo:lo + CONV_ROWS, :]
        mu = jnp.mean(acc, axis=-1, keepdims=True)
        cen = acc - mu
        var = jnp.mean(cen * cen, axis=-1, keepdims=True)
        y = (cen * lax.rsqrt(var + EPS)) * lng_ref[...] + lnb_ref[...]
        o_ref[c * CONV_ROWS:(c + 1) * CONV_ROWS, :] = (y * _sigmoid(y)).astype(o_ref.dtype)


def _conv_module(proj, w, b, ln_g, ln_b, batch, seq):
    t = proj.shape[0]
    nt = seq // CONV_TILE
    hb = CONV_TILE // CONV_HALO
    ca, cg = OFF_CA // CONV_CH, OFF_CG // CONV_CH
    cur = lambda col: pl.BlockSpec((CONV_TILE, CONV_CH), lambda bb, i, col=col: (bb * nt + i, col))
    halo = lambda col: pl.BlockSpec(
        (CONV_HALO, CONV_CH), lambda bb, i, col=col: (jnp.maximum((bb * nt + i) * hb - 1, 0), col))
    vec = pl.BlockSpec((1, CONV_CH), lambda bb, i: (0, 0))
    return pl.pallas_call(
        _conv_kernel,
        grid=(batch, nt),
        in_specs=[cur(ca), cur(cg), halo(ca), halo(cg),
                  pl.BlockSpec((CONV_HALO, CONV_CH), lambda bb, i: (0, 0)), vec, vec, vec],
        out_specs=pl.BlockSpec((CONV_TILE, CONV_CH), lambda bb, i: (bb * nt + i, 0)),
        out_shape=jax.ShapeDtypeStruct((t, CONV_CH), BF16),
        scratch_shapes=[pltpu.VMEM((CONV_HALO + CONV_TILE, CONV_CH), F32)],
        compiler_params=_cparams(("arbitrary", "arbitrary")),
    )(proj, proj, proj, proj, w, b, ln_g, ln_b)


def _mlstm_kernel(q_ref, k_ref, v_ref, om_ref, gate_ref, cw_ref, cb_ref, gb_ref, o_ref,
                  ext, ct_s, n_s, m_s, *, batch):
    c = pl.program_id(0)
    lc, dh, nh = MLSTM_CHUNK, MLSTM_DH, MLSTM_HEADS
    pad = SUBLANES

    @pl.when(c == 0)
    def _():
        ext[:, 0:pad, :] = jnp.zeros((batch, pad, 2 * MLSTM_W), F32)
        ct_s[...] = jnp.zeros_like(ct_s)
        n_s[...] = jnp.zeros_like(n_s)
        m_s[...] = jnp.zeros_like(m_s)

    @pl.when(c > 0)
    def _():
        ext[:, 0:pad, :] = ext[:, lc:lc + pad, :]

    ext[:, pad:, 0:MLSTM_W] = q_ref[...]
    ext[:, pad:, MLSTM_W:] = k_ref[...]

    row = lax.broadcasted_iota(I32, (lc, lc), 0)
    col = lax.broadcasted_iota(I32, (lc, lc), 1)
    causal = row >= col
    tril = causal.astype(F32)
    is_f = jnp.logical_and(col >= nh, col < 2 * nh)

    for b in range(batch):
        acc = jnp.broadcast_to(cb_ref[...], (lc, 2 * MLSTM_W))
        for k in range(MLSTM_CONV_K):
            lo = pad - (MLSTM_CONV_K - 1) + k
            acc = acc + cw_ref[k:k + 1, :] * ext[b, lo:lo + lc, :]
        qk = acc * _sigmoid(acc)

        gpre = gate_ref[b] + gb_ref[...]
        lf = jnp.where(is_f, _log_sigmoid(gpre), jnp.zeros_like(gpre))
        bcum = jnp.dot(tril, lf, precision=HIGHEST, preferred_element_type=F32)
        gpre_t = gpre.T
        bcum_t = bcum.T

        for h in range(nh):
            qh = qk[:, h * dh:(h + 1) * dh]
            kh = qk[:, MLSTM_W + h * dh:MLSTM_W + (h + 1) * dh] * (dh ** -0.5)
            vh = v_ref[b, :, h * dh:(h + 1) * dh]
            qb, kb, vb = qh.astype(BF16), kh.astype(BF16), vh.astype(BF16)
            bq = bcum[:, nh + h:nh + h + 1]
            br = bcum_t[nh + h:nh + h + 1, :]
            igr = gpre_t[h:h + 1, :]
            igc = gpre[:, h:h + 1]
            m_prev = m_s[b, h][:, 0:1]
            ct = ct_s[b, h]
            n_row = n_s[b, h]

            dmat = jnp.where(causal, (bq - br) + igr, NEG)
            a_inter = bq + m_prev
            m_t = jnp.maximum(a_inter, jnp.max(dmat, axis=-1, keepdims=True))
            w_intra = jnp.exp(dmat - m_t)
            sc = lax.dot_general(qb, kb, (((1,), (1,)), ((), ())), preferred_element_type=F32) * w_intra
            s_inter = jnp.exp(a_inter - m_t)
            num = (jnp.dot(sc.astype(BF16), vb, preferred_element_type=F32)
                   + s_inter * jnp.dot(qb, ct.astype(BF16), preferred_element_type=F32))
            den = (jnp.sum(sc, axis=-1, keepdims=True)
                   + s_inter * jnp.sum(qh * n_row, axis=-1, keepdims=True))
            hout = num / jnp.maximum(jnp.abs(den), jnp.exp(-m_t))
            og = _sigmoid(om_ref[b, :, h * dh:(h + 1) * dh])
            o_ref[b, :, h * dh:(h + 1) * dh] = (og * hout).astype(o_ref.dtype)

            gtot = br[:, lc - 1:lc]
            wst = (gtot - bq) + igc
            m_loc = jnp.max(wst, axis=0, keepdims=True)
            ek = jnp.exp(wst - m_loc) * kh
            c_loc = jnp.dot(ek.T.astype(BF16), vb, preferred_element_type=F32)
            n_loc = jnp.sum(ek, axis=0, keepdims=True)
            m_new = jnp.maximum(gtot + m_prev, m_loc)
            fa = jnp.exp(gtot + m_prev - m_new)
            fb = jnp.exp(m_loc - m_new)
            ct_s[b, h] = fa * ct + fb * c_loc
            n_s[b, h] = fa * n_row + fb * n_loc
            m_s[b, h] = jnp.broadcast_to(m_new, (1, LANES))


def _mlstm(proj3, cw, cb, gb, batch, seq):
    nc = seq // MLSTM_CHUNK
    wide = lambda off: pl.BlockSpec((batch, MLSTM_CHUNK, MLSTM_W), lambda c, off=off: (0, c, off // MLSTM_W))
    full = lambda shape: pl.BlockSpec(shape, lambda c: (0,) * len(shape))
    return pl.pallas_call(
        functools.partial(_mlstm_kernel, batch=batch),
        grid=(nc,),
        in_specs=[wide(OFF_QM), wide(OFF_KM), wide(OFF_VM), wide(OFF_OM),
                  pl.BlockSpec((batch, MLSTM_CHUNK, LANES), lambda c: (0, c, OFF_IF // LANES)),
                  full((SUBLANES, 2 * MLSTM_W)), full((1, 2 * MLSTM_W)), full((1, LANES))],
        out_specs=pl.BlockSpec((batch, MLSTM_CHUNK, MLSTM_W), lambda c: (0, c, 0)),
        out_shape=jax.ShapeDtypeStruct((batch, seq, MLSTM_W), BF16),
        scratch_shapes=[pltpu.VMEM((batch, MLSTM_CHUNK + SUBLANES, 2 * MLSTM_W), F32),
                        pltpu.VMEM((batch, MLSTM_HEADS, MLSTM_DH, MLSTM_DH), F32),
                        pltpu.VMEM((batch, MLSTM_HEADS, 1, MLSTM_DH), F32),
                        pltpu.VMEM((batch, MLSTM_HEADS, 1, LANES), F32)],
        compiler_params=_cparams(("arbitrary",)),
    )(proj3, proj3, proj3, proj3, proj3, cw, cb, gb)


def _merge_kernel(x_ref, gate_ref, ya_ref, yc_ref, ym_ref, wa_ref, wc_ref, wm_ref, wo_ref,
                  gf_ref, wr_ref, br_ref, xo_ref, h2_ref, lg_ref, *, tm):
    d = D_MODEL
    merged = (_sigmoid(gate_ref[:, 0:d]) * jnp.dot(ya_ref[...], wa_ref[...], preferred_element_type=F32)
              + _sigmoid(gate_ref[:, d:2 * d]) * jnp.dot(yc_ref[...], wc_ref[...], preferred_element_type=F32)
              + _sigmoid(gate_ref[:, 2 * d:3 * d]) * jnp.dot(ym_ref[...], wm_ref[...], preferred_element_type=F32))
    xn = x_ref[...] + jnp.dot(merged.astype(BF16), wo_ref[...], preferred_element_type=F32)
    xo_ref[...] = xn
    h2 = _rms(xn, gf_ref[...])
    lg_ref[...] = jnp.dot(h2, wr_ref[...], precision=HIGHEST, preferred_element_type=F32) + br_ref[...]
    for s in range(ROW_TILES):
        h2_ref[pl.ds(s, tm, stride=ROW_TILES), :] = h2[:, s * LANES:(s + 1) * LANES]


def _merge(x2, proj, y_att, y_conv, y_m, wa, wc, wm, wo, g_ffn, w_r, b_r, tm=256):
    t = x2.shape[0]
    rows = lambda n: pl.BlockSpec((tm, n), lambda i: (i, 0))
    full = lambda a: pl.BlockSpec(a.shape, lambda i: (0, 0))
    return pl.pallas_call(
        functools.partial(_merge_kernel, tm=tm),
        grid=(t // tm,),
        in_specs=[rows(D_MODEL), rows(3 * D_MODEL), rows(ATT_KV), rows(CONV_CH), rows(MLSTM_W),
                  full(wa), full(wc), full(wm), full(wo), full(g_ffn), full(w_r), full(b_r)],
        out_specs=[rows(D_MODEL), pl.BlockSpec((tm * ROW_TILES, LANES), lambda i: (i, 0)), rows(LANES)],
        out_shape=[jax.ShapeDtypeStruct((t, D_MODEL), F32),
                   jax.ShapeDtypeStruct((t * ROW_TILES, LANES), F32),
                   jax.ShapeDtypeStruct((t, LANES), F32)],
        compiler_params=_cparams(("arbitrary",)),
    )(x2, proj, y_att, y_conv, y_m, wa, wc, wm, wo, g_ffn, w_r, b_r)


ROUTER_TILE = 512


def _router_kernel(lg_ref, idx_ref, wt_ref, cnt_ref, run):
    n = ROUTER_TILE
    ng, ne = N_GROUPS_MOE, EXPERTS_PER_GROUP

    @pl.when(pl.program_id(0) == 0)
    def _():
        run[...] = jnp.zeros_like(run)

    lt = lg_ref[...].T
    lgrp = lt[0:ng, :]
    gmax = jnp.max(lgrp, axis=0, keepdims=True)
    grow = lax.broadcasted_iota(I32, (ng, n), 0)
    g_sel = jnp.min(jnp.where(lgrp == gmax, grow, ng), axis=0, keepdims=True)
    p_g = 1.0 / jnp.sum(jnp.exp(lgrp - gmax), axis=0, keepdims=True)

    le_sel = jnp.zeros((ne, n), F32)
    for g in range(ng):
        le_sel = jnp.where(g_sel == g, lt[(g + 1) * ne:(g + 2) * ne, :], le_sel)
    erow = lax.broadcasted_iota(I32, (ne, n), 0)
    v1 = jnp.max(le_sel, axis=0, keepdims=True)
    i1 = jnp.min(jnp.where(le_sel == v1, erow, ne), axis=0, keepdims=True)
    rest = jnp.where(erow == i1, -jnp.inf, le_sel)
    v2 = jnp.max(rest, axis=0, keepdims=True)
    i2 = jnp.min(jnp.where(rest == v2, erow, ne), axis=0, keepdims=True)
    e2 = jnp.exp(v2 - v1)
    w1 = p_g * (1.0 / (1.0 + e2))
    w2 = p_g * (e2 / (1.0 + e2))
    eid1 = g_sel * ne + i1
    eid2 = g_sel * ne + i2

    xrow = lax.broadcasted_iota(I32, (N_EXPERTS, n), 0)
    hit1 = xrow == eid1
    hit2 = xrow == eid2
    onehot = jnp.where(hit1, 1.0, 0.0) + jnp.where(hit2, 1.0, 0.0)
    r_i = lax.broadcasted_iota(I32, (n, n), 0)
    c_i = lax.broadcasted_iota(I32, (n, n), 1)
    before = jnp.where(r_i < c_i, 1.0, 0.0).astype(BF16)
    cnt = jnp.dot(onehot.astype(BF16), before, preferred_element_type=F32) + run[:, 0:1]
    rank1 = jnp.sum(jnp.where(hit1, cnt, 0.0), axis=0, keepdims=True)
    rank2 = jnp.sum(jnp.where(hit2, cnt, 0.0), axis=0, keepdims=True)
    run[...] = run[...] + jnp.sum(onehot, axis=1, keepdims=True)

    zi = jnp.zeros((SUBLANES - 4, n), I32)
    idx_ref[...] = jnp.concatenate([eid1, eid2, rank1.astype(I32), rank2.astype(I32), zi], axis=0)
    wt_ref[...] = jnp.concatenate([w1, w2, jnp.zeros((SUBLANES - 2, n), F32)], axis=0)
    cnt_ref[...] = run[...]


def _router(logits):
    t = logits.shape[0]
    n = ROUTER_TILE
    return pl.pallas_call(
        _router_kernel,
        grid=(t // n,),
        in_specs=[pl.BlockSpec((n, LANES), lambda i: (i, 0))],
        out_specs=[pl.BlockSpec((SUBLANES, n), lambda i: (0, i)),
                   pl.BlockSpec((SUBLANES, n), lambda i: (0, i)),
                   pl.BlockSpec((N_EXPERTS, LANES), lambda i: (0, 0))],
        out_shape=[jax.ShapeDtypeStruct((SUBLANES, t), I32),
                   jax.ShapeDtypeStruct((SUBLANES, t), F32),
                   jax.ShapeDtypeStruct((N_EXPERTS, LANES), F32)],
        scratch_shapes=[pltpu.VMEM((N_EXPERTS, LANES), F32)],
        compiler_params=_cparams(("arbitrary",)),
    )(logits)


DISPATCH_TILE = 512


def _dispatch_kernel(dest_ref, x_ref, xs_in_ref, xs_ref, sem):
    del xs_in_ref
    n = DISPATCH_TILE

    def row_copy(t, slot):
        d = pl.multiple_of(dest_ref[0, 0, 2 * t + slot], SUBLANES)
        src = x_ref.at[pl.ds(pl.multiple_of(t * ROW_TILES, ROW_TILES), ROW_TILES), :]
        return pltpu.make_async_copy(src, xs_ref.at[pl.ds(d, ROW_TILES), :], sem)

    def issue(t, carry):
        row_copy(t, 0).start()
        row_copy(t, 1).start()
        return carry

    lax.fori_loop(0, n, issue, 0)

    def drain(t, carry):
        row_copy(t, 0).wait()
        row_copy(t, 1).wait()
        return carry

    lax.fori_loop(0, n, drain, 0)


def _dispatch(h2t, dest_tiles, p_len):
    t = h2t.shape[0] // ROW_TILES
    n = DISPATCH_TILE
    xs0 = jnp.zeros((p_len * ROW_TILES, LANES), F32)
    return pl.pallas_call(
        _dispatch_kernel,
        grid=(t // n,),
        in_specs=[pl.BlockSpec((1, 1, 2 * n), lambda i: (i, 0, 0), memory_space=pltpu.SMEM),
                  pl.BlockSpec((n * ROW_TILES, LANES), lambda i: (i, 0)),
                  pl.BlockSpec(memory_space=pl.ANY)],
        out_specs=pl.BlockSpec(memory_space=pl.ANY),
        out_shape=jax.ShapeDtypeStruct((p_len * ROW_TILES, LANES), F32),
        scratch_shapes=[pltpu.SemaphoreType.DMA(())],
        input_output_aliases={2: 0},
        compiler_params=_cparams(("arbitrary",)),
    )(dest_tiles, h2t, xs0)


def _expert_kernel(blk_ref, xs_ref, wg_ref, wu_ref, wd_ref, y_ref):
    del blk_ref
    xb = jnp.concatenate(
        [xs_ref[pl.ds(s, MOE_BLOCK, stride=ROW_TILES), :] for s in range(ROW_TILES)], axis=-1).astype(BF16)
    hg = jnp.dot(xb, wg_ref[...], preferred_element_type=F32)
    hu = jnp.dot(xb, wu_ref[...], preferred_element_type=F32)
    hb = (hg * _sigmoid(hg)) * hu
    y = jnp.dot(hb.astype(BF16), wd_ref[...], preferred_element_type=F32)
    for s in range(ROW_TILES):
        y_ref[pl.ds(s, MOE_BLOCK, stride=ROW_TILES), :] = y[:, s * LANES:(s + 1) * LANES]


def _experts(blk_e, xs, wg, wu, wd):
    nb = blk_e.shape[0]
    rows = pl.BlockSpec((MOE_BLOCK * ROW_TILES, LANES), lambda i, be: (i, 0))
    return pl.pallas_call(
        _expert_kernel,
        grid_spec=pltpu.PrefetchScalarGridSpec(
            num_scalar_prefetch=1,
            grid=(nb,),
            in_specs=[rows,
                      pl.BlockSpec((None, D_MODEL, D_EXPERT), lambda i, be: (be[i], 0, 0)),
                      pl.BlockSpec((None, D_MODEL, D_EXPERT), lambda i, be: (be[i], 0, 0)),
                      pl.BlockSpec((None, D_EXPERT, D_MODEL), lambda i, be: (be[i], 0, 0))],
            out_specs=rows),
        out_shape=jax.ShapeDtypeStruct(xs.shape, F32),
        compiler_params=_cparams(("arbitrary",)),
    )(blk_e, xs, wg, wu, wd)


COMBINE_TILE = 256


def _combine_kernel(dest_ref, x_ref, wt_ref, p_ref, y_hbm, gp_ref, wpg_ref, wpp_ref, gfin_ref, o_ref,
                    ybuf, sem, *, final):
    n = COMBINE_TILE

    def row_copy(t, slot):
        d = pl.multiple_of(dest_ref[0, 0, 2 * t + slot], SUBLANES)
        dst = ybuf.at[slot, pl.ds(pl.multiple_of(t * ROW_TILES, ROW_TILES), ROW_TILES), :]
        return pltpu.make_async_copy(y_hbm.at[pl.ds(d, ROW_TILES), :], dst, sem)

    def issue(t, carry):
        row_copy(t, 0).start()
        row_copy(t, 1).start()
        return carry

    lax.fori_loop(0, n, issue, 0)
    pe = jnp.dot(p_ref[...].astype(BF16), wpp_ref[...], preferred_element_type=F32)

    def drain(t, carry):
        row_copy(t, 0).wait()
        row_copy(t, 1).wait()
        return carry

    lax.fori_loop(0, n, drain, 0)

    def rows_of(slot):
        return jnp.concatenate(
            [ybuf[slot, pl.ds(s, n, stride=ROW_TILES), :] for s in range(ROW_TILES)], axis=-1)

    xn = x_ref[...] + (wt_ref[:, 0:1] * rows_of(0) + wt_ref[:, 1:2] * rows_of(1))
    hp = _rms(xn, gp_ref[...])
    gate = _sigmoid(jnp.dot(hp.astype(BF16), wpg_ref[...], preferred_element_type=F32))
    xo = xn + gate * pe
    if final:
        xo = _rms(xo, gfin_ref[...])
    o_ref[...] = xo


def _combine(dest_tiles, x2, wts, p2, y, g_ple, wpg, wpp, g_final, final):
    t = x2.shape[0]
    n = COMBINE_TILE
    rows = lambda w: pl.BlockSpec((n, w), lambda i: (i, 0))
    full = lambda a: pl.BlockSpec(a.shape, lambda i: (0, 0))
    return pl.pallas_call(
        functools.partial(_combine_kernel, final=final),
        grid=(t // n,),
        in_specs=[pl.BlockSpec((1, 1, 2 * n), lambda i: (i, 0, 0), memory_space=pltpu.SMEM),
                  rows(D_MODEL), rows(2), rows(PLE_DIM),
                  pl.BlockSpec(memory_space=pl.ANY),
                  full(g_ple), full(wpg), full(wpp), full(g_final)],
        out_specs=rows(D_MODEL),
        out_shape=jax.ShapeDtypeStruct((t, D_MODEL), F32),
        scratch_shapes=[pltpu.VMEM((2, n * ROW_TILES, LANES), F32), pltpu.SemaphoreType.DMA(())],
        compiler_params=_cparams(("arbitrary",)),
    )(dest_tiles, x2, wts, p2, y, g_ple, wpg, wpp, g_final)


def _moe_plan(idx, cnt, t):
    counts = cnt[:, 0].astype(I32)
    padded = (counts + MOE_BLOCK - 1) // MOE_BLOCK * MOE_BLOCK
    ends_p = jnp.cumsum(padded)
    pstart = ends_p - padded
    dest = (pstart[idx[0:2]] + idx[2:4]) * ROW_TILES
    p_len = 2 * t + N_EXPERTS * MOE_BLOCK
    nb = p_len // MOE_BLOCK
    blk_e = jnp.minimum(jnp.searchsorted(ends_p, jnp.arange(nb, dtype=I32) * MOE_BLOCK, side="right"),
                        N_EXPERTS - 1).astype(I32)
    return dest.T, blk_e, p_len


def _tiles(dest_t, n):
    t = dest_t.shape[0]
    return dest_t.reshape(t // n, 1, 2 * n)


def kernel(x, p, g_mix, w_in, conv_w, conv_b, conv_ln_g, conv_ln_b, mlstm_conv_w, mlstm_conv_b,
           mlstm_i_bias, mlstm_f_bias, w_br_att, w_br_conv, w_br_mlstm, w_out, g_ffn,
           w_router_g, b_router_g, w_router_e, b_router_e, w_exp_gate, w_exp_up, w_exp_down,
           g_ple, w_ple_gate, w_ple_proj, g_final):
    batch, seq, d = x.shape
    depth = w_in.shape[0]
    t = batch * seq
    bias_tab = jnp.asarray(_att_bias_table())
    x2 = x.reshape(t, d)
    nr = N_GROUPS_MOE + N_EXPERTS
    for i in range(depth):
        w = w_in[i]
        wp = jnp.concatenate([w[:, ORIG_SMALL + 2 * MLSTM_HEADS:], w[:, :ORIG_SMALL],
                              w[:, ORIG_SMALL:ORIG_SMALL + 2 * MLSTM_HEADS],
                              jnp.zeros((d, LANES - 2 * MLSTM_HEADS), F32)], axis=1).astype(BF16)
        proj = _inproj(x2, g_mix[i][None, :], wp)

        y_att = _attention(proj, bias_tab, batch, seq)
        cw = jnp.concatenate([conv_w[i], jnp.zeros((CONV_HALO - CONV_K, CONV_CH), F32)], axis=0)
        y_conv = _conv_module(proj, cw, conv_b[i][None, :], conv_ln_g[i][None, :], conv_ln_b[i][None, :],
                              batch, seq)
        mw = jnp.concatenate([mlstm_conv_w[i], jnp.zeros((SUBLANES - MLSTM_CONV_K, 2 * MLSTM_W), F32)], axis=0)
        gb = jnp.concatenate([mlstm_i_bias[i], mlstm_f_bias[i],
                              jnp.zeros((LANES - 2 * MLSTM_HEADS,), F32)])[None, :]
        y_m = _mlstm(proj.reshape(batch, seq, PROJ_N), mw, mlstm_conv_b[i][None, :], gb, batch, seq)

        gpad = EXPERTS_PER_GROUP - N_GROUPS_MOE
        w_r = jnp.concatenate([w_router_g[i], jnp.zeros((d, gpad), F32), w_router_e[i],
                               jnp.zeros((d, LANES - nr - gpad), F32)], axis=1)
        b_r = jnp.concatenate([b_router_g[i], jnp.zeros((gpad,), F32), b_router_e[i],
                               jnp.zeros((LANES - nr - gpad,), F32)])[None, :]
        x2, h2t, logits = _merge(x2, proj, y_att, y_conv, y_m.reshape(t, MLSTM_W),
                                 w_br_att[i].astype(BF16), w_br_conv[i].astype(BF16),
                                 w_br_mlstm[i].astype(BF16), w_out[i].astype(BF16),
                                 g_ffn[i][None, :], w_r, b_r)

        idx, wt, cnt = _router(logits)
        dest_t, blk_e, p_len = _moe_plan(idx, cnt, t)
        xs = _dispatch(h2t, _tiles(dest_t, DISPATCH_TILE), p_len)
        ys = _experts(blk_e, xs, w_exp_gate[i].astype(BF16), w_exp_up[i].astype(BF16),
                      w_exp_down[i].astype(BF16))
        x2 = _combine(_tiles(dest_t, COMBINE_TILE), x2, wt[0:2].T, p[i].reshape(t, PLE_DIM), ys,
                      g_ple[i][None, :], w_ple_gate[i].astype(BF16), w_ple_proj[i].astype(BF16),
                      g_final[None, :], final=(i == depth - 1))
    return x2.reshape(batch, seq, d)
```

```python
import functools

import numpy as np
import jax
import jax.numpy as jnp
from jax import lax
from jax.experimental import pallas as pl
from jax.experimental.pallas import tpu as pltpu

F32 = jnp.float32
BF16 = jnp.bfloat16
I32 = jnp.int32
HIGHEST = lax.Precision.HIGHEST

D_MODEL = 1024
HEAD_DIM = 64
ATT_SLOTS = 8
DILATIONS = (1, 4, 16)
N_DIL = 3
ATT_BLOCK = 128
ATT_Q = ATT_SLOTS * N_DIL * HEAD_DIM
ATT_KV = ATT_SLOTS * HEAD_DIM
ALIBI_MAX_EXP = 8.0
CONV_CH = 512
CONV_K = 31
MLSTM_HEADS = 4
MLSTM_DH = 128
MLSTM_W = MLSTM_HEADS * MLSTM_DH
MLSTM_CONV_K = 4
MLSTM_CHUNK = 128
N_GROUPS_MOE = 4
EXPERTS_PER_GROUP = 8
N_EXPERTS = N_GROUPS_MOE * EXPERTS_PER_GROUP
D_EXPERT = 256
MOE_BLOCK = 128
PLE_DIM = 256
EPS = 1e-6
NEG = -1e30

LANES = 128
SUBLANES = 8
ROW_TILES = D_MODEL // LANES
VMEM_LIMIT = 56 * 1024 * 1024

OFF_GATE = 0
OFF_QA = 3 * D_MODEL
OFF_KA = OFF_QA + ATT_Q
OFF_VA = OFF_KA + ATT_KV
OFF_CA = OFF_VA + ATT_KV
OFF_CG = OFF_CA + CONV_CH
OFF_QM = OFF_CG + CONV_CH
OFF_KM = OFF_QM + MLSTM_W
OFF_VM = OFF_KM + MLSTM_W
OFF_OM = OFF_VM + MLSTM_W
OFF_IF = OFF_OM + MLSTM_W
PROJ_N = OFF_IF + LANES
PROJ_TN = PROJ_N // 3
ORIG_SMALL = ATT_Q + 2 * ATT_KV + 2 * CONV_CH + 4 * MLSTM_W

ATT_TILE = ATT_BLOCK * DILATIONS[-1]
ATT_UNROLL = 16


def _cparams(sem, vmem=VMEM_LIMIT):
    return pltpu.CompilerParams(dimension_semantics=sem, vmem_limit_bytes=vmem)


def _rms(x, g):
    r = lax.rsqrt(jnp.mean(x * x, axis=-1, keepdims=True) + EPS)
    return (x * r) * g


def _sigmoid(x):
    return 1.0 / (1.0 + jnp.exp(-x))


def _log_sigmoid(x):
    return jnp.minimum(x, 0.0) - jnp.log(1.0 + jnp.exp(-jnp.abs(x)))


def _inproj_kernel(x_ref, g_ref, w_ref, o_ref, h_scr):
    @pl.when(pl.program_id(1) == 0)
    def _():
        h_scr[...] = _rms(x_ref[...], g_ref[...]).astype(BF16)

    o_ref[...] = jnp.dot(h_scr[...], w_ref[...], preferred_element_type=F32)


def _inproj(x2, g, w, tm=512):
    t = x2.shape[0]
    return pl.pallas_call(
        _inproj_kernel,
        grid=(t // tm, PROJ_N // PROJ_TN),
        in_specs=[pl.BlockSpec((tm, D_MODEL), lambda i, j: (i, 0)),
                  pl.BlockSpec((1, D_MODEL), lambda i, j: (0, 0)),
                  pl.BlockSpec((D_MODEL, PROJ_TN), lambda i, j: (0, j))],
        out_specs=pl.BlockSpec((tm, PROJ_TN), lambda i, j: (i, j)),
        out_shape=jax.ShapeDtypeStruct((t, PROJ_N), F32),
        scratch_shapes=[pltpu.VMEM((tm, D_MODEL), BF16)],
        compiler_params=_cparams(("arbitrary", "arbitrary")),
    )(x2, g, w)


def _att_bias_table():
    j = np.arange(1, ATT_SLOTS * N_DIL + 1, dtype=np.float64)
    slopes = (2.0 ** (-ALIBI_MAX_EXP * j / (ATT_SLOTS * N_DIL))).reshape(N_DIL, ATT_SLOTS)
    qi = np.arange(ATT_BLOCK)[:, None] + ATT_BLOCK
    ki = np.arange(2 * ATT_BLOCK)[None, :]
    dist = qi - ki
    valid = (dist >= 0) & (dist <= ATT_BLOCK)
    tab = np.zeros((ATT_SLOTS // 2, N_DIL * 4, ATT_BLOCK, 2 * ATT_BLOCK), np.float32)
    for hp in range(ATT_SLOTS // 2):
        for g, dil in enumerate(DILATIONS):
            for h in range(2):
                slope = np.float32(slopes[g, 2 * hp + h])
                bias = -(slope * (dil * dist).astype(np.float32))
                tab[hp, g * 4 + h * 2 + 0] = np.where(valid, bias, NEG)
                tab[hp, g * 4 + h * 2 + 1] = np.where(valid & (ki >= ATT_BLOCK), bias, NEG)
    return tab


def _att_kernel(q0_ref, q1_ref, q2_ref, kc_ref, kp_ref, vc_ref, vp_ref, bias_ref, o_ref,
                qs, ks0, ks1, ks2, vs0, vs1, vs2, og, lg, onat, lnat, tmp):
    i = pl.program_id(2)
    q_refs = (q0_ref, q1_ref, q2_ref)
    k_scr = (ks0, ks1, ks2)
    v_scr = (vs0, vs1, vs2)
    scale = HEAD_DIM ** -0.5

    for g, dil in enumerate(DILATIONS[:2]):
        lt = ATT_TILE // dil
        cs = ATT_BLOCK + lt
        for r in range(dil):
            cur = pl.ds(r, lt, stride=dil) if dil > 1 else pl.ds(0, lt)
            prv = (pl.ds(ATT_TILE - ATT_BLOCK * dil + r, ATT_BLOCK, stride=dil) if dil > 1
                   else pl.ds(ATT_TILE - ATT_BLOCK, ATT_BLOCK))
            qs[g, r * lt:(r + 1) * lt, :] = (q_refs[g][cur, :] * scale).astype(BF16)
            k_scr[g][r * cs + ATT_BLOCK:(r + 1) * cs, :] = kc_ref[cur, :].astype(BF16)
            k_scr[g][r * cs:r * cs + ATT_BLOCK, :] = kp_ref[prv, :].astype(BF16)
            v_scr[g][r * cs + ATT_BLOCK:(r + 1) * cs, :] = vc_ref[cur, :].astype(BF16)
            v_scr[g][r * cs:r * cs + ATT_BLOCK, :] = vp_ref[prv, :].astype(BF16)

    sub, quarter = DILATIONS[1], ATT_TILE // DILATIONS[1]

    def classes16(src_ref, put, mul=None):
        for r4 in range(sub):
            tmp[r4 * quarter:(r4 + 1) * quarter, :] = src_ref[pl.ds(r4, quarter, stride=sub), :]
        for r in range(DILATIONS[2]):
            v = tmp[pl.ds((r % sub) * quarter + r // sub, ATT_BLOCK, stride=sub), :]
            put(r, (v if mul is None else v * mul).astype(BF16))

    cs16 = 2 * ATT_BLOCK

    def put_q(r, v):
        qs[2, r * ATT_BLOCK:(r + 1) * ATT_BLOCK, :] = v

    def put_at(scr, off):
        def put(r, v):
            scr[r * cs16 + off:r * cs16 + off + ATT_BLOCK, :] = v
        return put

    classes16(q2_ref, put_q, scale)
    classes16(kc_ref, put_at(ks2, ATT_BLOCK))
    classes16(kp_ref, put_at(ks2, 0))
    classes16(vc_ref, put_at(vs2, ATT_BLOCK))
    classes16(vp_ref, put_at(vs2, 0))

    lane = lax.broadcasted_iota(I32, (ATT_BLOCK, LANES), 1)
    head0 = lane < HEAD_DIM

    for g, dil in enumerate(DILATIONS):
        nj = ATT_TILE // dil // ATT_BLOCK

        def block(t, carry, g=g, nj=nj):
            r = t // nj
            j = t - r * nj
            qoff = pl.multiple_of(t * ATT_BLOCK, ATT_BLOCK)
            koff = pl.multiple_of((r * (nj + 1) + j) * ATT_BLOCK, ATT_BLOCK)
            first = jnp.logical_and(i == 0, j == 0).astype(I32)
            qb = qs[g, pl.ds(qoff, ATT_BLOCK), :]
            kb = k_scr[g][pl.ds(koff, 2 * ATT_BLOCK), :]
            vb = v_scr[g][pl.ds(koff, 2 * ATT_BLOCK), :]
            outs, lses = [], []
            for h in range(2):
                qh = jnp.where(head0 if h == 0 else jnp.logical_not(head0), qb, jnp.zeros_like(qb))
                s = lax.dot_general(qh, kb, (((1,), (1,)), ((), ())), preferred_element_type=F32)
                s = s + bias_ref[g * 4 + h * 2 + first]
                m = jnp.max(s, axis=-1, keepdims=True)
                p = jnp.exp(s - m)
                den = jnp.sum(p, axis=-1, keepdims=True)
                o = jnp.dot(p.astype(BF16), vb, preferred_element_type=F32) / den
                outs.append(o)
                lses.append(jnp.broadcast_to(m + jnp.log(den), (ATT_BLOCK, LANES)))
            og[g, pl.ds(qoff, ATT_BLOCK), :] = jnp.where(head0, outs[0], outs[1])
            lg[g, pl.ds(qoff, ATT_BLOCK), :] = jnp.where(head0, lses[0], lses[1])
            return carry

        lax.fori_loop(0, ATT_TILE // ATT_BLOCK, block, 0, unroll=ATT_UNROLL)

    for r in range(sub):
        onat[0, pl.ds(r, quarter, stride=sub), :] = og[1, r * quarter:(r + 1) * quarter, :]
        lnat[0, pl.ds(r, quarter, stride=sub), :] = lg[1, r * quarter:(r + 1) * quarter, :]
    for src, dst in ((og, onat), (lg, lnat)):
        for r in range(DILATIONS[2]):
            tmp[pl.ds((r % sub) * quarter + r // sub, ATT_BLOCK, stride=sub), :] = (
                src[2, r * ATT_BLOCK:(r + 1) * ATT_BLOCK, :])
        for r4 in range(sub):
            dst[1, pl.ds(r4, quarter, stride=sub), :] = tmp[r4 * quarter:(r4 + 1) * quarter, :]

    rc = 256

    def merge(c, carry):
        rows = pl.ds(pl.multiple_of(c * rc, rc), rc)
        l0, l1, l2 = lg[0, rows, :], lnat[0, rows, :], lnat[1, rows, :]
        mx = jnp.maximum(jnp.maximum(l0, l1), l2)
        w0, w1, w2 = jnp.exp(l0 - mx), jnp.exp(l1 - mx), jnp.exp(l2 - mx)
        tot = w0 + w1 + w2
        o = (w0 / tot) * og[0, rows, :] + (w1 / tot) * onat[0, rows, :] + (w2 / tot) * onat[1, rows, :]
        o_ref[rows, :] = o.astype(o_ref.dtype)
        return carry

    lax.fori_loop(0, ATT_TILE // rc, merge, 0)


def _attention(proj, bias_tab, batch, seq):
    t = proj.shape[0]
    nt = seq // ATT_TILE
    cb = lambda off: off // LANES

    def cur(col):
        return pl.BlockSpec((ATT_TILE, LANES), lambda hp, b, i, col=col: (b * nt + i, col + hp))

    def prev(col):
        return pl.BlockSpec((ATT_TILE, LANES),
                            lambda hp, b, i, col=col: (b * nt + jnp.maximum(i - 1, 0), col + hp))

    npairs = ATT_SLOTS // 2
    kv_rows = [ATT_TILE + ATT_BLOCK * d for d in DILATIONS]
    return pl.pallas_call(
        _att_kernel,
        grid=(npairs, batch, nt),
        in_specs=[cur(cb(OFF_QA)), cur(cb(OFF_QA) + npairs), cur(cb(OFF_QA) + 2 * npairs),
                  cur(cb(OFF_KA)), prev(cb(OFF_KA)), cur(cb(OFF_VA)), prev(cb(OFF_VA)),
                  pl.BlockSpec((None, N_DIL * 4, ATT_BLOCK, 2 * ATT_BLOCK), lambda hp, b, i: (hp, 0, 0, 0))],
        out_specs=pl.BlockSpec((ATT_TILE, LANES), lambda hp, b, i: (b * nt + i, hp)),
        out_shape=jax.ShapeDtypeStruct((t, ATT_KV), BF16),
        scratch_shapes=([pltpu.VMEM((N_DIL, ATT_TILE, LANES), BF16)]
                        + [pltpu.VMEM((n, LANES), BF16) for n in kv_rows]
                        + [pltpu.VMEM((n, LANES), BF16) for n in kv_rows]
                        + [pltpu.VMEM((N_DIL, ATT_TILE, LANES), F32)] * 2
                        + [pltpu.VMEM((N_DIL - 1, ATT_TILE, LANES), F32)] * 2
                        + [pltpu.VMEM((ATT_TILE, LANES), F32)]),
        compiler_params=_cparams(("arbitrary", "arbitrary", "arbitrary")),
    )(proj, proj, proj, proj, proj, proj, proj, bias_tab)


CONV_TILE = 256
CONV_HALO = 32
CONV_ROWS = 32


def _conv_kernel(a_ref, gt_ref, ha_ref, hg_ref, w_ref, b_ref, lng_ref, lnb_ref, o_ref, ybuf):
    i = pl.program_id(1)
    halo = ha_ref[...] * _sigmoid(hg_ref[...])
    ybuf[0:CONV_HALO, :] = jnp.where(i > 0, halo, jnp.zeros_like(halo))
    ybuf[CONV_HALO:, :] = a_ref[...] * _sigmoid(gt_ref[...])
    first = CONV_HALO - (CONV_K - 1)
    for c in range(CONV_TILE // CONV_ROWS):
        acc = jnp.broadcast_to(b_ref[...], (CONV_ROWS, CONV_CH))
        for k in range(CONV_K):
            lo = c * CONV_ROWS + first + k
            acc = acc + w_ref[k:k + 1, :] * ybuf[lo:lo + CONV_ROWS, :]
        mu = jnp.mean(acc, axis=-1, keepdims=True)
        cen = acc - mu
        var = jnp.mean(cen * cen, axis=-1, keepdims=True)
        y = (cen * lax.rsqrt(var + EPS)) * lng_ref[...] + lnb_ref[...]
        o_ref[c * CONV_ROWS:(c + 1) * CONV_ROWS, :] = (y * _sigmoid(y)).astype(o_ref.dtype)


def _conv_module(proj, w, b, ln_g, ln_b, batch, seq):
    t = proj.shape[0]
    nt = seq // CONV_TILE
    hb = CONV_TILE // CONV_HALO
    ca, cg = OFF_CA // CONV_CH, OFF_CG // CONV_CH
    cur = lambda col: pl.BlockSpec((CONV_TILE, CONV_CH), lambda bb, i, col=col: (bb * nt + i, col))
    halo = lambda col: pl.BlockSpec(
        (CONV_HALO, CONV_CH), lambda bb, i, col=col: (jnp.maximum((bb * nt + i) * hb - 1, 0), col))
    vec = pl.BlockSpec((1, CONV_CH), lambda bb, i: (0, 0))
    return pl.pallas_call(
        _conv_kernel,
        grid=(batch, nt),
        in_specs=[cur(ca), cur(cg), halo(ca), halo(cg),
                  pl.BlockSpec((CONV_HALO, CONV_CH), lambda bb, i: (0, 0)), vec, vec, vec],
        out_specs=pl.BlockSpec((CONV_TILE, CONV_CH), lambda bb, i: (bb * nt + i, 0)),
        out_shape=jax.ShapeDtypeStruct((t, CONV_CH), BF16),
        scratch_shapes=[pltpu.VMEM((CONV_HALO + CONV_TILE, CONV_CH), F32)],
        compiler_params=_cparams(("arbitrary", "arbitrary")),
    )(proj, proj, proj, proj, w, b, ln_g, ln_b)


def _mlstm_kernel(q_ref, k_ref, v_ref, om_ref, gate_ref, cw_ref, cb_ref, gb_ref, o_ref,
                  ext, ct_s, n_s, m_s, *, batch):
    c = pl.program_id(0)
    lc, dh, nh = MLSTM_CHUNK, MLSTM_DH, MLSTM_HEADS
    pad = SUBLANES

    @pl.when(c == 0)
    def _():
        ext[:, 0:pad, :] = jnp.zeros((batch, pad, 2 * MLSTM_W), F32)
        ct_s[...] = jnp.zeros_like(ct_s)
        n_s[...] = jnp.zeros_like(n_s)
        m_s[...] = jnp.zeros_like(m_s)

    @pl.when(c > 0)
    def _():
        ext[:, 0:pad, :] = ext[:, lc:lc + pad, :]

    ext[:, pad:, 0:MLSTM_W] = q_ref[...]
    ext[:, pad:, MLSTM_W:] = k_ref[...]

    row = lax.broadcasted_iota(I32, (lc, lc), 0)
    col = lax.broadcasted_iota(I32, (lc, lc), 1)
    causal = row >= col
    tril = causal.astype(F32)
    is_f = jnp.logical_and(col >= nh, col < 2 * nh)

    for b in range(batch):
        acc = jnp.broadcast_to(cb_ref[...], (lc, 2 * MLSTM_W))
        for k in range(MLSTM_CONV_K):
            lo = pad - (MLSTM_CONV_K - 1) + k
            acc = acc + cw_ref[k:k + 1, :] * ext[b, lo:lo + lc, :]
        qk = acc * _sigmoid(acc)

        gpre = gate_ref[b] + gb_ref[...]
        lf = jnp.where(is_f, _log_sigmoid(gpre), jnp.zeros_like(gpre))
        bcum = jnp.dot(tril, lf, precision=HIGHEST, preferred_element_type=F32)
        gpre_t = gpre.T
        bcum_t = bcum.T

        for h in range(nh):
            qh = qk[:, h * dh:(h + 1) * dh]
            kh = qk[:, MLSTM_W + h * dh:MLSTM_W + (h + 1) * dh] * (dh ** -0.5)
            vh = v_ref[b, :, h * dh:(h + 1) * dh]
            qb, kb, vb = qh.astype(BF16), kh.astype(BF16), vh.astype(BF16)
            bq = bcum[:, nh + h:nh + h + 1]
            br = bcum_t[nh + h:nh + h + 1, :]
            igr = gpre_t[h:h + 1, :]
            igc = gpre[:, h:h + 1]
            m_prev = m_s[b, h][:, 0:1]
            ct = ct_s[b, h]
            n_row = n_s[b, h]

            dmat = jnp.where(causal, (bq - br) + igr, NEG)
            a_inter = bq + m_prev
            m_t = jnp.maximum(a_inter, jnp.max(dmat, axis=-1, keepdims=True))
            w_intra = jnp.exp(dmat - m_t)
            sc = lax.dot_general(qb, kb, (((1,), (1,)), ((), ())), preferred_element_type=F32) * w_intra
            s_inter = jnp.exp(a_inter - m_t)
            num = (jnp.dot(sc.astype(BF16), vb, preferred_element_type=F32)
                   + s_inter * jnp.dot(qb, ct.astype(BF16), preferred_element_type=F32))
            den = (jnp.sum(sc, axis=-1, keepdims=True)
                   + s_inter * jnp.sum(qh * n_row, axis=-1, keepdims=True))
            hout = num / jnp.maximum(jnp.abs(den), jnp.exp(-m_t))
            og = _sigmoid(om_ref[b, :, h * dh:(h + 1) * dh])
            o_ref[b, :, h * dh:(h + 1) * dh] = (og * hout).astype(o_ref.dtype)

            gtot = br[:, lc - 1:lc]
            wst = (gtot - bq) + igc
            m_loc = jnp.max(wst, axis=0, keepdims=True)
            ek = jnp.exp(wst - m_loc) * kh
            c_loc = jnp.dot(ek.T.astype(BF16), vb, preferred_element_type=F32)
            n_loc = jnp.sum(ek, axis=0, keepdims=True)
            m_new = jnp.maximum(gtot + m_prev, m_loc)
            fa = jnp.exp(gtot + m_prev - m_new)
            fb = jnp.exp(m_loc - m_new)
            ct_s[b, h] = fa * ct + fb * c_loc
            n_s[b, h] = fa * n_row + fb * n_loc
            m_s[b, h] = jnp.broadcast_to(m_new, (1, LANES))


def _mlstm(proj3, cw, cb, gb, batch, seq):
    nc = seq // MLSTM_CHUNK
    wide = lambda off: pl.BlockSpec((batch, MLSTM_CHUNK, MLSTM_W), lambda c, off=off: (0, c, off // MLSTM_W))
    full = lambda shape: pl.BlockSpec(shape, lambda c: (0,) * len(shape))
    return pl.pallas_call(
        functools.partial(_mlstm_kernel, batch=batch),
        grid=(nc,),
        in_specs=[wide(OFF_QM), wide(OFF_KM), wide(OFF_VM), wide(OFF_OM),
                  pl.BlockSpec((batch, MLSTM_CHUNK, LANES), lambda c: (0, c, OFF_IF // LANES)),
                  full((SUBLANES, 2 * MLSTM_W)), full((1, 2 * MLSTM_W)), full((1, LANES))],
        out_specs=pl.BlockSpec((batch, MLSTM_CHUNK, MLSTM_W), lambda c: (0, c, 0)),
        out_shape=jax.ShapeDtypeStruct((batch, seq, MLSTM_W), BF16),
        scratch_shapes=[pltpu.VMEM((batch, MLSTM_CHUNK + SUBLANES, 2 * MLSTM_W), F32),
                        pltpu.VMEM((batch, MLSTM_HEADS, MLSTM_DH, MLSTM_DH), F32),
                        pltpu.VMEM((batch, MLSTM_HEADS, 1, MLSTM_DH), F32),
                        pltpu.VMEM((batch, MLSTM_HEADS, 1, LANES), F32)],
        compiler_params=_cparams(("arbitrary",)),
    )(proj3, proj3, proj3, proj3, proj3, cw, cb, gb)


def _merge_kernel(x_ref, gate_ref, ya_ref, yc_ref, ym_ref, wa_ref, wc_ref, wm_ref, wo_ref,
                  gf_ref, wr_ref, br_ref, xo_ref, h2_ref, lg_ref, *, tm):
    d = D_MODEL
    merged = (_sigmoid(gate_ref[:, 0:d]) * jnp.dot(ya_ref[...], wa_ref[...], preferred_element_type=F32)
              + _sigmoid(gate_ref[:, d:2 * d]) * jnp.dot(yc_ref[...], wc_ref[...], preferred_element_type=F32)
              + _sigmoid(gate_ref[:, 2 * d:3 * d]) * jnp.dot(ym_ref[...], wm_ref[...], preferred_element_type=F32))
    xn = x_ref[...] + jnp.dot(merged.astype(BF16), wo_ref[...], preferred_element_type=F32)
    xo_ref[...] = xn
    h2 = _rms(xn, gf_ref[...])
    lg_ref[...] = jnp.dot(h2, wr_ref[...], precision=HIGHEST, preferred_element_type=F32) + br_ref[...]
    for s in range(ROW_TILES):
        h2_ref[pl.ds(s, tm, stride=ROW_TILES), :] = h2[:, s * LANES:(s + 1) * LANES]


def _merge(x2, proj, y_att, y_conv, y_m, wa, wc, wm, wo, g_ffn, w_r, b_r, tm=256):
    t = x2.shape[0]
    rows = lambda n: pl.BlockSpec((tm, n), lambda i: (i, 0))
    full = lambda a: pl.BlockSpec(a.shape, lambda i: (0, 0))
    return pl.pallas_call(
        functools.partial(_merge_kernel, tm=tm),
        grid=(t // tm,),
        in_specs=[rows(D_MODEL), rows(3 * D_MODEL), rows(ATT_KV), rows(CONV_CH), rows(MLSTM_W),
                  full(wa), full(wc), full(wm), full(wo), full(g_ffn), full(w_r), full(b_r)],
        out_specs=[rows(D_MODEL), pl.BlockSpec((tm * ROW_TILES, LANES), lambda i: (i, 0)), rows(LANES)],
        out_shape=[jax.ShapeDtypeStruct((t, D_MODEL), F32),
                   jax.ShapeDtypeStruct((t * ROW_TILES, LANES), F32),
                   jax.ShapeDtypeStruct((t, LANES), F32)],
        compiler_params=_cparams(("arbitrary",)),
    )(x2, proj, y_att, y_conv, y_m, wa, wc, wm, wo, g_ffn, w_r, b_r)


ROUTER_TILE = 512


def _router_kernel(lg_ref, idx_ref, wt_ref, cnt_ref, run):
    n = ROUTER_TILE
    ng, ne = N_GROUPS_MOE, EXPERTS_PER_GROUP

    @pl.when(pl.program_id(0) == 0)
    def _():
        run[...] = jnp.zeros_like(run)

    lt = lg_ref[...].T
    lgrp = lt[0:ng, :]
    gmax = jnp.max(lgrp, axis=0, keepdims=True)
    grow = lax.broadcasted_iota(I32, (ng, n), 0)
    g_sel = jnp.min(jnp.where(lgrp == gmax, grow, ng), axis=0, keepdims=True)
    p_g = 1.0 / jnp.sum(jnp.exp(lgrp - gmax), axis=0, keepdims=True)

    le_sel = jnp.zeros((ne, n), F32)
    for g in range(ng):
        le_sel = jnp.where(g_sel == g, lt[(g + 1) * ne:(g + 2) * ne, :], le_sel)
    erow = lax.broadcasted_iota(I32, (ne, n), 0)
    v1 = jnp.max(le_sel, axis=0, keepdims=True)
    i1 = jnp.min(jnp.where(le_sel == v1, erow, ne), axis=0, keepdims=True)
    rest = jnp.where(erow == i1, -jnp.inf, le_sel)
    v2 = jnp.max(rest, axis=0, keepdims=True)
    i2 = jnp.min(jnp.where(rest == v2, erow, ne), axis=0, keepdims=True)
    e2 = jnp.exp(v2 - v1)
    w1 = p_g * (1.0 / (1.0 + e2))
    w2 = p_g * (e2 / (1.0 + e2))
    eid1 = g_sel * ne + i1
    eid2 = g_sel * ne + i2

    xrow = lax.broadcasted_iota(I32, (N_EXPERTS, n), 0)
    hit1 = xrow == eid1
    hit2 = xrow == eid2
    onehot = jnp.where(hit1, 1.0, 0.0) + jnp.where(hit2, 1.0, 0.0)
    r_i = lax.broadcasted_iota(I32, (n, n), 0)
    c_i = lax.broadcasted_iota(I32, (n, n), 1)
    before = jnp.where(r_i < c_i, 1.0, 0.0).astype(BF16)
    cnt = jnp.dot(onehot.astype(BF16), before, preferred_element_type=F32) + run[:, 0:1]
    rank1 = jnp.sum(jnp.where(hit1, cnt, 0.0), axis=0, keepdims=True)
    rank2 = jnp.sum(jnp.where(hit2, cnt, 0.0), axis=0, keepdims=True)
    run[...] = run[...] + jnp.sum(onehot, axis=1, keepdims=True)

    zi = jnp.zeros((SUBLANES - 4, n), I32)
    idx_ref[...] = jnp.concatenate([eid1, eid2, rank1.astype(I32), rank2.astype(I32), zi], axis=0)
    wt_ref[...] = jnp.concatenate([w1, w2, jnp.zeros((SUBLANES - 2, n), F32)], axis=0)
    cnt_ref[...] = run[...]


def _router(logits):
    t = logits.shape[0]
    n = ROUTER_TILE
    return pl.pallas_call(
        _router_kernel,
        grid=(t // n,),
        in_specs=[pl.BlockSpec((n, LANES), lambda i: (i, 0))],
        out_specs=[pl.BlockSpec((SUBLANES, n), lambda i: (0, i)),
                   pl.BlockSpec((SUBLANES, n), lambda i: (0, i)),
                   pl.BlockSpec((N_EXPERTS, LANES), lambda i: (0, 0))],
        out_shape=[jax.ShapeDtypeStruct((SUBLANES, t), I32),
                   jax.ShapeDtypeStruct((SUBLANES, t), F32),
                   jax.ShapeDtypeStruct((N_EXPERTS, LANES), F32)],
        scratch_shapes=[pltpu.VMEM((N_EXPERTS, LANES), F32)],
        compiler_params=_cparams(("arbitrary",)),
    )(logits)


DISPATCH_TILE = 512


def _dispatch_kernel(dest_ref, x_ref, xs_in_ref, xs_ref, sem):
    del xs_in_ref
    n = DISPATCH_TILE

    def row_copy(t, slot):
        d = pl.multiple_of(dest_ref[0, 0, 2 * t + slot], SUBLANES)
        src = x_ref.at[pl.ds(pl.multiple_of(t * ROW_TILES, ROW_TILES), ROW_TILES), :]
        return pltpu.make_async_copy(src, xs_ref.at[pl.ds(d, ROW_TILES), :], sem)

    def issue(t, carry):
        row_copy(t, 0).start()
        row_copy(t, 1).start()
        return carry

    lax.fori_loop(0, n, issue, 0)

    def drain(t, carry):
        row_copy(t, 0).wait()
        row_copy(t, 1).wait()
        return carry

    lax.fori_loop(0, n, drain, 0)


def _dispatch(h2t, dest_tiles, p_len):
    t = h2t.shape[0] // ROW_TILES
    n = DISPATCH_TILE
    xs0 = jnp.zeros((p_len * ROW_TILES, LANES), F32)
    return pl.pallas_call(
        _dispatch_kernel,
        grid=(t // n,),
        in_specs=[pl.BlockSpec((1, 1, 2 * n), lambda i: (i, 0, 0), memory_space=pltpu.SMEM),
                  pl.BlockSpec((n * ROW_TILES, LANES), lambda i: (i, 0)),
                  pl.BlockSpec(memory_space=pl.ANY)],
        out_specs=pl.BlockSpec(memory_space=pl.ANY),
        out_shape=jax.ShapeDtypeStruct((p_len * ROW_TILES, LANES), F32),
        scratch_shapes=[pltpu.SemaphoreType.DMA(())],
        input_output_aliases={2: 0},
        compiler_params=_cparams(("arbitrary",)),
    )(dest_tiles, h2t, xs0)


def _expert_kernel(blk_ref, xs_ref, wg_ref, wu_ref, wd_ref, y_ref):
    del blk_ref
    xb = jnp.concatenate(
        [xs_ref[pl.ds(s, MOE_BLOCK, stride=ROW_TILES), :] for s in range(ROW_TILES)], axis=-1).astype(BF16)
    hg = jnp.dot(xb, wg_ref[...], preferred_element_type=F32)
    hu = jnp.dot(xb, wu_ref[...], preferred_element_type=F32)
    hb = (hg * _sigmoid(hg)) * hu
    y = jnp.dot(hb.astype(BF16), wd_ref[...], preferred_element_type=F32)
    for s in range(ROW_TILES):
        y_ref[pl.ds(s, MOE_BLOCK, stride=ROW_TILES), :] = y[:, s * LANES:(s + 1) * LANES]


def _experts(blk_e, xs, wg, wu, wd):
    nb = blk_e.shape[0]
    rows = pl.BlockSpec((MOE_BLOCK * ROW_TILES, LANES), lambda i, be: (i, 0))
    return pl.pallas_call(
        _expert_kernel,
        grid_spec=pltpu.PrefetchScalarGridSpec(
            num_scalar_prefetch=1,
            grid=(nb,),
            in_specs=[rows,
                      pl.BlockSpec((None, D_MODEL, D_EXPERT), lambda i, be: (be[i], 0, 0)),
                      pl.BlockSpec((None, D_MODEL, D_EXPERT), lambda i, be: (be[i], 0, 0)),
                      pl.BlockSpec((None, D_EXPERT, D_MODEL), lambda i, be: (be[i], 0, 0))],
            out_specs=rows),
        out_shape=jax.ShapeDtypeStruct(xs.shape, F32),
        compiler_params=_cparams(("arbitrary",)),
    )(blk_e, xs, wg, wu, wd)


COMBINE_TILE = 256


def _combine_kernel(dest_ref, x_ref, wt_ref, p_ref, y_hbm, gp_ref, wpg_ref, wpp_ref, gfin_ref, o_ref,
                    ybuf, sem, *, final):
    n = COMBINE_TILE

    def row_copy(t, slot):
        d = pl.multiple_of(dest_ref[0, 0, 2 * t + slot], SUBLANES)
        dst = ybuf.at[slot, pl.ds(pl.multiple_of(t * ROW_TILES, ROW_TILES), ROW_TILES), :]
        return pltpu.make_async_copy(y_hbm.at[pl.ds(d, ROW_TILES), :], dst, sem)

    def issue(t, carry):
        row_copy(t, 0).start()
        row_copy(t, 1).start()
        return carry

    lax.fori_loop(0, n, issue, 0)
    pe = jnp.dot(p_ref[...].astype(BF16), wpp_ref[...], preferred_element_type=F32)

    def drain(t, carry):
        row_copy(t, 0).wait()
        row_copy(t, 1).wait()
        return carry

    lax.fori_loop(0, n, drain, 0)

    def rows_of(slot):
        return jnp.concatenate(
            [ybuf[slot, pl.ds(s, n, stride=ROW_TILES), :] for s in range(ROW_TILES)], axis=-1)

    xn = x_ref[...] + (wt_ref[:, 0:1] * rows_of(0) + wt_ref[:, 1:2] * rows_of(1))
    hp = _rms(xn, gp_ref[...])
    gate = _sigmoid(jnp.dot(hp.astype(BF16), wpg_ref[...], preferred_element_type=F32))
    xo = xn + gate * pe
    if final:
        xo = _rms(xo, gfin_ref[...])
    o_ref[...] = xo


def _combine(dest_tiles, x2, wts, p2, y, g_ple, wpg, wpp, g_final, final):
    t = x2.shape[0]
    n = COMBINE_TILE
    rows = lambda w: pl.BlockSpec((n, w), lambda i: (i, 0))
    full = lambda a: pl.BlockSpec(a.shape, lambda i: (0, 0))
    return pl.pallas_call(
        functools.partial(_combine_kernel, final=final),
        grid=(t // n,),
        in_specs=[pl.BlockSpec((1, 1, 2 * n), lambda i: (i, 0, 0), memory_space=pltpu.SMEM),
                  rows(D_MODEL), rows(2), rows(PLE_DIM),
                  pl.BlockSpec(memory_space=pl.ANY),
                  full(g_ple), full(wpg), full(wpp), full(g_final)],
        out_specs=rows(D_MODEL),
        out_shape=jax.ShapeDtypeStruct((t, D_MODEL), F32),
        scratch_shapes=[pltpu.VMEM((2, n * ROW_TILES, LANES), F32), pltpu.SemaphoreType.DMA(())],
        compiler_params=_cparams(("arbitrary",)),
    )(dest_tiles, x2, wts, p2, y, g_ple, wpg, wpp, g_final)


def _moe_plan(idx, cnt, t):
    counts = cnt[:, 0].astype(I32)
    padded = (counts + MOE_BLOCK - 1) // MOE_BLOCK * MOE_BLOCK
    ends_p = jnp.cumsum(padded)
    pstart = ends_p - padded
    is_e = idx[0:2][:, :, None] == jnp.arange(N_EXPERTS, dtype=I32)
    dest = (jnp.sum(jnp.where(is_e, pstart, 0), axis=-1) + idx[2:4]) * ROW_TILES
    p_len = 2 * t + N_EXPERTS * MOE_BLOCK
    nb = p_len // MOE_BLOCK
    blk_start = jnp.arange(nb, dtype=I32) * MOE_BLOCK
    blk_e = jnp.minimum(jnp.sum((ends_p[None, :] <= blk_start[:, None]).astype(I32), axis=1), N_EXPERTS - 1)
    return dest.T, blk_e, p_len


def _tiles(dest_t, n):
    t = dest_t.shape[0]
    return dest_t.reshape(t // n, 1, 2 * n)


def kernel(x, p, g_mix, w_in, conv_w, conv_b, conv_ln_g, conv_ln_b, mlstm_conv_w, mlstm_conv_b,
           mlstm_i_bias, mlstm_f_bias, w_br_att, w_br_conv, w_br_mlstm, w_out, g_ffn,
           w_router_g, b_router_g, w_router_e, b_router_e, w_exp_gate, w_exp_up, w_exp_down,
           g_ple, w_ple_gate, w_ple_proj, g_final):
    batch, seq, d = x.shape
    depth = w_in.shape[0]
    t = batch * seq
    bias_tab = jnp.asarray(_att_bias_table())
    x2 = x.reshape(t, d)
    nr = N_GROUPS_MOE + N_EXPERTS
    for i in range(depth):
        w = w_in[i]
        wp = jnp.concatenate([w[:, ORIG_SMALL + 2 * MLSTM_HEADS:], w[:, :ORIG_SMALL],
                              w[:, ORIG_SMALL:ORIG_SMALL + 2 * MLSTM_HEADS],
                              jnp.zeros((d, LANES - 2 * MLSTM_HEADS), F32)], axis=1).astype(BF16)
        proj = _inproj(x2, g_mix[i][None, :], wp)

        y_att = _attention(proj, bias_tab, batch, seq)
        cw = jnp.concatenate([conv_w[i], jnp.zeros((CONV_HALO - CONV_K, CONV_CH), F32)], axis=0)
        y_conv = _conv_module(proj, cw, conv_b[i][None, :], conv_ln_g[i][None, :], conv_ln_b[i][None, :],
                              batch, seq)
        mw = jnp.concatenate([mlstm_conv_w[i], jnp.zeros((SUBLANES - MLSTM_CONV_K, 2 * MLSTM_W), F32)], axis=0)
        gb = jnp.concatenate([mlstm_i_bias[i], mlstm_f_bias[i],
                              jnp.zeros((LANES - 2 * MLSTM_HEADS,), F32)])[None, :]
        y_m = _mlstm(proj.reshape(batch, seq, PROJ_N), mw, mlstm_conv_b[i][None, :], gb, batch, seq)

        gpad = EXPERTS_PER_GROUP - N_GROUPS_MOE
        w_r = jnp.concatenate([w_router_g[i], jnp.zeros((d, gpad), F32), w_router_e[i],
                               jnp.zeros((d, LANES - nr - gpad), F32)], axis=1)
        b_r = jnp.concatenate([b_router_g[i], jnp.zeros((gpad,), F32), b_router_e[i],
                               jnp.zeros((LANES - nr - gpad,), F32)])[None, :]
        x2, h2t, logits = _merge(x2, proj, y_att, y_conv, y_m.reshape(t, MLSTM_W),
                                 w_br_att[i].astype(BF16), w_br_conv[i].astype(BF16),
                                 w_br_mlstm[i].astype(BF16), w_out[i].astype(BF16),
                                 g_ffn[i][None, :], w_r, b_r)

        idx, wt, cnt = _router(logits)
        dest_t, blk_e, p_len = _moe_plan(idx, cnt, t)
        xs = _dispatch(h2t, _tiles(dest_t, DISPATCH_TILE), p_len)
        ys = _experts(blk_e, xs, w_exp_gate[i].astype(BF16), w_exp_up[i].astype(BF16),
                      w_exp_down[i].astype(BF16))
        x2 = _combine(_tiles(dest_t, COMBINE_TILE), x2, wt[0:2].T, p[i].reshape(t, PLE_DIM), ys,
                      g_ple[i][None, :], w_ple_gate[i].astype(BF16), w_ple_proj[i].astype(BF16),
                      g_final[None, :], final=(i == depth - 1))
    return x2.reshape(batch, seq, d)
```

```python
import functools

import numpy as np
import jax
import jax.numpy as jnp
from jax import lax
from jax.experimental import pallas as pl
from jax.experimental.pallas import tpu as pltpu

F32 = jnp.float32
BF16 = jnp.bfloat16
I32 = jnp.int32
HIGHEST = lax.Precision.HIGHEST

D_MODEL = 1024
HEAD_DIM = 64
ATT_SLOTS = 8
DILATIONS = (1, 4, 16)
N_DIL = 3
ATT_BLOCK = 128
ATT_Q = ATT_SLOTS * N_DIL * HEAD_DIM
ATT_KV = ATT_SLOTS * HEAD_DIM
ALIBI_MAX_EXP = 8.0
CONV_CH = 512
CONV_K = 31
MLSTM_HEADS = 4
MLSTM_DH = 128
MLSTM_W = MLSTM_HEADS * MLSTM_DH
MLSTM_CONV_K = 4
MLSTM_CHUNK = 128
N_GROUPS_MOE = 4
EXPERTS_PER_GROUP = 8
N_EXPERTS = N_GROUPS_MOE * EXPERTS_PER_GROUP
D_EXPERT = 256
MOE_BLOCK = 128
PAIRS_PER_GROUP = EXPERTS_PER_GROUP * (EXPERTS_PER_GROUP - 1) // 2
N_CLASSES = N_GROUPS_MOE * PAIRS_PER_GROUP
PLE_DIM = 256
EPS = 1e-6
NEG = -1e30

LANES = 128
SUBLANES = 8
ROW_TILES = D_MODEL // LANES
VMEM_LIMIT = 56 * 1024 * 1024

OFF_GATE = 0
OFF_QA = 3 * D_MODEL
OFF_KA = OFF_QA + ATT_Q
OFF_VA = OFF_KA + ATT_KV
OFF_CA = OFF_VA + ATT_KV
OFF_CG = OFF_CA + CONV_CH
OFF_QM = OFF_CG + CONV_CH
OFF_KM = OFF_QM + MLSTM_W
OFF_VM = OFF_KM + MLSTM_W
OFF_OM = OFF_VM + MLSTM_W
OFF_IF = OFF_OM + MLSTM_W
PROJ_N = OFF_IF + LANES
PROJ_TN = PROJ_N // 3
ORIG_SMALL = ATT_Q + 2 * ATT_KV + 2 * CONV_CH + 4 * MLSTM_W

ATT_TILE = ATT_BLOCK * DILATIONS[-1]
ATT_UNROLL = 16


def _cparams(sem, vmem=VMEM_LIMIT):
    return pltpu.CompilerParams(dimension_semantics=sem, vmem_limit_bytes=vmem)


def _rms(x, g):
    r = lax.rsqrt(jnp.mean(x * x, axis=-1, keepdims=True) + EPS)
    return (x * r) * g


def _sigmoid(x):
    return 1.0 / (1.0 + jnp.exp(-x))


def _log_sigmoid(x):
    return jnp.minimum(x, 0.0) - jnp.log(1.0 + jnp.exp(-jnp.abs(x)))


def _inproj_kernel(x_ref, g_ref, w_ref, o_ref):
    h = _rms(x_ref[...], g_ref[...]).astype(BF16)
    for j in range(PROJ_N // PROJ_TN):
        cols = slice(j * PROJ_TN, (j + 1) * PROJ_TN)
        o_ref[:, cols] = jnp.dot(h, w_ref[:, cols], preferred_element_type=F32)


def _inproj(x2, g, w, tm=256):
    t = x2.shape[0]
    return pl.pallas_call(
        _inproj_kernel,
        grid=(t // tm,),
        in_specs=[pl.BlockSpec((tm, D_MODEL), lambda i: (i, 0)),
                  pl.BlockSpec((1, D_MODEL), lambda i: (0, 0)),
                  pl.BlockSpec((D_MODEL, PROJ_N), lambda i: (0, 0), pipeline_mode=pl.Buffered(1))],
        out_specs=pl.BlockSpec((tm, PROJ_N), lambda i: (i, 0)),
        out_shape=jax.ShapeDtypeStruct((t, PROJ_N), F32),
        compiler_params=_cparams(("arbitrary",)),
    )(x2, g, w)


def _att_bias_table():
    j = np.arange(1, ATT_SLOTS * N_DIL + 1, dtype=np.float64)
    slopes = (2.0 ** (-ALIBI_MAX_EXP * j / (ATT_SLOTS * N_DIL))).reshape(N_DIL, ATT_SLOTS)
    qi = np.arange(ATT_BLOCK)[:, None] + ATT_BLOCK
    ki = np.arange(2 * ATT_BLOCK)[None, :]
    dist = qi - ki
    valid = (dist >= 0) & (dist <= ATT_BLOCK)
    tab = np.zeros((ATT_SLOTS // 2, N_DIL * 4, ATT_BLOCK, 2 * ATT_BLOCK), np.float32)
    for hp in range(ATT_SLOTS // 2):
        for g, dil in enumerate(DILATIONS):
            for h in range(2):
                slope = np.float32(slopes[g, 2 * hp + h])
                bias = -(slope * (dil * dist).astype(np.float32))
                tab[hp, g * 4 + h * 2 + 0] = np.where(valid, bias, NEG)
                tab[hp, g * 4 + h * 2 + 1] = np.where(valid & (ki >= ATT_BLOCK), bias, NEG)
    return tab


def _att_kernel(q0_ref, q1_ref, q2_ref, kc_ref, kp_ref, vc_ref, vp_ref, bias_ref, o_ref,
                qs, ks0, ks1, ks2, vs0, vs1, vs2, og, lg, onat, lnat, tmp):
    i = pl.program_id(2)
    q_refs = (q0_ref, q1_ref, q2_ref)
    k_scr = (ks0, ks1, ks2)
    v_scr = (vs0, vs1, vs2)
    scale = HEAD_DIM ** -0.5

    for g, dil in enumerate(DILATIONS[:2]):
        lt = ATT_TILE // dil
        cs = ATT_BLOCK + lt
        for r in range(dil):
            cur = pl.ds(r, lt, stride=dil) if dil > 1 else pl.ds(0, lt)
            prv = (pl.ds(ATT_TILE - ATT_BLOCK * dil + r, ATT_BLOCK, stride=dil) if dil > 1
                   else pl.ds(ATT_TILE - ATT_BLOCK, ATT_BLOCK))
            qs[g, r * lt:(r + 1) * lt, :] = (q_refs[g][cur, :] * scale).astype(BF16)
            k_scr[g][r * cs + ATT_BLOCK:(r + 1) * cs, :] = kc_ref[cur, :].astype(BF16)
            k_scr[g][r * cs:r * cs + ATT_BLOCK, :] = kp_ref[prv, :].astype(BF16)
            v_scr[g][r * cs + ATT_BLOCK:(r + 1) * cs, :] = vc_ref[cur, :].astype(BF16)
            v_scr[g][r * cs:r * cs + ATT_BLOCK, :] = vp_ref[prv, :].astype(BF16)

    sub, quarter = DILATIONS[1], ATT_TILE // DILATIONS[1]

    def classes16(src_ref, put, mul=None):
        for r4 in range(sub):
            tmp[r4 * quarter:(r4 + 1) * quarter, :] = src_ref[pl.ds(r4, quarter, stride=sub), :]
        for r in range(DILATIONS[2]):
            v = tmp[pl.ds((r % sub) * quarter + r // sub, ATT_BLOCK, stride=sub), :]
            put(r, (v if mul is None else v * mul).astype(BF16))

    cs16 = 2 * ATT_BLOCK

    def put_q(r, v):
        qs[2, r * ATT_BLOCK:(r + 1) * ATT_BLOCK, :] = v

    def put_at(scr, off):
        def put(r, v):
            scr[r * cs16 + off:r * cs16 + off + ATT_BLOCK, :] = v
        return put

    classes16(q2_ref, put_q, scale)
    classes16(kc_ref, put_at(ks2, ATT_BLOCK))
    classes16(kp_ref, put_at(ks2, 0))
    classes16(vc_ref, put_at(vs2, ATT_BLOCK))
    classes16(vp_ref, put_at(vs2, 0))

    lane = lax.broadcasted_iota(I32, (ATT_BLOCK, LANES), 1)
    head0 = lane < HEAD_DIM

    for g, dil in enumerate(DILATIONS):
        nj = ATT_TILE // dil // ATT_BLOCK

        def block(t, carry, g=g, nj=nj):
            r = t // nj
            j = t - r * nj
            qoff = pl.multiple_of(t * ATT_BLOCK, ATT_BLOCK)
            koff = pl.multiple_of((r * (nj + 1) + j) * ATT_BLOCK, ATT_BLOCK)
            first = jnp.logical_and(i == 0, j == 0).astype(I32)
            qb = qs[g, pl.ds(qoff, ATT_BLOCK), :]
            kb = k_scr[g][pl.ds(koff, 2 * ATT_BLOCK), :]
            vb = v_scr[g][pl.ds(koff, 2 * ATT_BLOCK), :]
            outs, lses = [], []
            for h in range(2):
                qh = jnp.where(head0 if h == 0 else jnp.logical_not(head0), qb, jnp.zeros_like(qb))
                s = lax.dot_general(qh, kb, (((1,), (1,)), ((), ())), preferred_element_type=F32)
                s = s + bias_ref[g * 4 + h * 2 + first]
                m = jnp.max(s, axis=-1, keepdims=True)
                p = jnp.exp(s - m)
                den = jnp.sum(p, axis=-1, keepdims=True)
                o = jnp.dot(p.astype(BF16), vb, preferred_element_type=F32) / den
                outs.append(o)
                lses.append(jnp.broadcast_to(m + jnp.log(den), (ATT_BLOCK, LANES)))
            og[g, pl.ds(qoff, ATT_BLOCK), :] = jnp.where(head0, outs[0], outs[1])
            lg[g, pl.ds(qoff, ATT_BLOCK), :] = jnp.where(head0, lses[0], lses[1])
            return carry

        lax.fori_loop(0, ATT_TILE // ATT_BLOCK, block, 0, unroll=ATT_UNROLL)

    for r in range(sub):
        onat[0, pl.ds(r, quarter, stride=sub), :] = og[1, r * quarter:(r + 1) * quarter, :]
        lnat[0, pl.ds(r, quarter, stride=sub), :] = lg[1, r * quarter:(r + 1) * quarter, :]
    for src, dst in ((og, onat), (lg, lnat)):
        for r in range(DILATIONS[2]):
            tmp[pl.ds((r % sub) * quarter + r // sub, ATT_BLOCK, stride=sub), :] = (
                src[2, r * ATT_BLOCK:(r + 1) * ATT_BLOCK, :])
        for r4 in range(sub):
            dst[1, pl.ds(r4, quarter, stride=sub), :] = tmp[r4 * quarter:(r4 + 1) * quarter, :]

    rc = 256

    def merge(c, carry):
        rows = pl.ds(pl.multiple_of(c * rc, rc), rc)
        l0, l1, l2 = lg[0, rows, :], lnat[0, rows, :], lnat[1, rows, :]
        mx = jnp.maximum(jnp.maximum(l0, l1), l2)
        w0, w1, w2 = jnp.exp(l0 - mx), jnp.exp(l1 - mx), jnp.exp(l2 - mx)
        tot = w0 + w1 + w2
        o = (w0 / tot) * og[0, rows, :] + (w1 / tot) * onat[0, rows, :] + (w2 / tot) * onat[1, rows, :]
        o_ref[rows, :] = o.astype(o_ref.dtype)
        return carry

    lax.fori_loop(0, ATT_TILE // rc, merge, 0)


def _attention(proj, bias_tab, batch, seq):
    t = proj.shape[0]
    nt = seq // ATT_TILE
    cb = lambda off: off // LANES

    def cur(col):
        return pl.BlockSpec((ATT_TILE, LANES), lambda hp, b, i, col=col: (b * nt + i, col + hp))

    def prev(col):
        return pl.BlockSpec((ATT_TILE, LANES),
                            lambda hp, b, i, col=col: (b * nt + jnp.maximum(i - 1, 0), col + hp))

    npairs = ATT_SLOTS // 2
    kv_rows = [ATT_TILE + ATT_BLOCK * d for d in DILATIONS]
    return pl.pallas_call(
        _att_kernel,
        grid=(npairs, batch, nt),
        in_specs=[cur(cb(OFF_QA)), cur(cb(OFF_QA) + npairs), cur(cb(OFF_QA) + 2 * npairs),
                  cur(cb(OFF_KA)), prev(cb(OFF_KA)), cur(cb(OFF_VA)), prev(cb(OFF_VA)),
                  pl.BlockSpec((None, N_DIL * 4, ATT_BLOCK, 2 * ATT_BLOCK), lambda hp, b, i: (hp, 0, 0, 0))],
        out_specs=pl.BlockSpec((ATT_TILE, LANES), lambda hp, b, i: (b * nt + i, hp)),
        out_shape=jax.ShapeDtypeStruct((t, ATT_KV), BF16),
        scratch_shapes=([pltpu.VMEM((N_DIL, ATT_TILE, LANES), BF16)]
                        + [pltpu.VMEM((n, LANES), BF16) for n in kv_rows]
                        + [pltpu.VMEM((n, LANES), BF16) for n in kv_rows]
                        + [pltpu.VMEM((N_DIL, ATT_TILE, LANES), F32)] * 2
                        + [pltpu.VMEM((N_DIL - 1, ATT_TILE, LANES), F32)] * 2
                        + [pltpu.VMEM((ATT_TILE, LANES), F32)]),
        compiler_params=_cparams(("arbitrary", "arbitrary", "arbitrary")),
    )(proj, proj, proj, proj, proj, proj, proj, bias_tab)


CONV_TILE = 256
CONV_HALO = 32
CONV_ROWS = 32


def _conv_kernel(a_ref, gt_ref, ha_ref, hg_ref, w_ref, b_ref, lng_ref, lnb_ref, o_ref, ybuf, win):
    i = pl.program_id(1)
    halo = ha_ref[...] * _sigmoid(hg_ref[...])
    ybuf[0:CONV_HALO, :] = jnp.where(i > 0, halo, jnp.zeros_like(halo))
    ybuf[CONV_HALO:, :] = a_ref[...] * _sigmoid(gt_ref[...])
    first = CONV_HALO - (CONV_K - 1)
    for c in range(CONV_TILE // CONV_ROWS):
        for j in range(SUBLANES):
            span = ((CONV_K - 1 - j) // SUBLANES) * SUBLANES + CONV_ROWS
            lo = c * CONV_ROWS + first + j
            win[j, 0:span, :] = ybuf[lo:lo + span, :]
        acc = jnp.broadcast_to(b_ref[...], (CONV_ROWS, CONV_CH))
        for k in range(CONV_K):
            j, a = k % SUBLANES, k // SUBLANES
            acc = acc + w_ref[k:k + 1, :] * win[j, a * SUBLANES:a * SUBLANES + CONV_ROWS, :]
        mu = jnp.mean(acc, axis=-1, keepdims=True)
        cen = acc - mu
        var = jnp.mean(cen * cen, axis=-1, keepdims=True)
        y = (cen * lax.rsqrt(var + EPS)) * lng_ref[...] + lnb_ref[...]
        o_ref[c * CONV_ROWS:(c + 1) * CONV_ROWS, :] = (y * _sigmoid(y)).astype(o_ref.dtype)


def _conv_module(proj, w, b, ln_g, ln_b, batch, seq):
    t = proj.shape[0]
    nt = seq // CONV_TILE
    hb = CONV_TILE // CONV_HALO
    ca, cg = OFF_CA // CONV_CH, OFF_CG // CONV_CH
    cur = lambda col: pl.BlockSpec((CONV_TILE, CONV_CH), lambda bb, i, col=col: (bb * nt + i, col))
    halo = lambda col: pl.BlockSpec(
        (CONV_HALO, CONV_CH), lambda bb, i, col=col: (jnp.maximum((bb * nt + i) * hb - 1, 0), col))
    vec = pl.BlockSpec((1, CONV_CH), lambda bb, i: (0, 0))
    return pl.pallas_call(
        _conv_kernel,
        grid=(batch, nt),
        in_specs=[cur(ca), cur(cg), halo(ca), halo(cg),
                  pl.BlockSpec((CONV_HALO, CONV_CH), lambda bb, i: (0, 0)), vec, vec, vec],
        out_specs=pl.BlockSpec((CONV_TILE, CONV_CH), lambda bb, i: (bb * nt + i, 0)),
        out_shape=jax.ShapeDtypeStruct((t, CONV_CH), BF16),
        scratch_shapes=[pltpu.VMEM((CONV_HALO + CONV_TILE, CONV_CH), F32),
                        pltpu.VMEM((SUBLANES, CONV_HALO + CONV_ROWS, CONV_CH), F32)],
        compiler_params=_cparams(("arbitrary", "arbitrary")),
    )(proj, proj, proj, proj, w, b, ln_g, ln_b)


def _mlstm_kernel(q_ref, k_ref, v_ref, om_ref, gate_ref, cw_ref, cb_ref, gb_ref, o_ref,
                  ext, ct_s, n_s, m_s, *, batch):
    c = pl.program_id(0)
    lc, dh, nh = MLSTM_CHUNK, MLSTM_DH, MLSTM_HEADS
    pad = SUBLANES

    @pl.when(c == 0)
    def _():
        ext[:, 0:pad, :] = jnp.zeros((batch, pad, 2 * MLSTM_W), F32)
        ct_s[...] = jnp.zeros_like(ct_s)
        n_s[...] = jnp.zeros_like(n_s)
        m_s[...] = jnp.zeros_like(m_s)

    @pl.when(c > 0)
    def _():
        ext[:, 0:pad, :] = ext[:, lc:lc + pad, :]

    ext[:, pad:, 0:MLSTM_W] = q_ref[...]
    ext[:, pad:, MLSTM_W:] = k_ref[...]

    row = lax.broadcasted_iota(I32, (lc, lc), 0)
    col = lax.broadcasted_iota(I32, (lc, lc), 1)
    causal = row >= col
    tril = causal.astype(F32)
    is_f = jnp.logical_and(col >= nh, col < 2 * nh)

    for b in range(batch):
        acc = jnp.broadcast_to(cb_ref[...], (lc, 2 * MLSTM_W))
        for k in range(MLSTM_CONV_K):
            lo = pad - (MLSTM_CONV_K - 1) + k
            acc = acc + cw_ref[k:k + 1, :] * ext[b, lo:lo + lc, :]
        qk = acc * _sigmoid(acc)

        gpre = gate_ref[b] + gb_ref[...]
        lf = jnp.where(is_f, _log_sigmoid(gpre), jnp.zeros_like(gpre))
        bcum = jnp.dot(tril, lf, precision=HIGHEST, preferred_element_type=F32)
        gpre_t = gpre.T
        bcum_t = bcum.T

        for h in range(nh):
            qh = qk[:, h * dh:(h + 1) * dh]
            kh = qk[:, MLSTM_W + h * dh:MLSTM_W + (h + 1) * dh] * (dh ** -0.5)
            vh = v_ref[b, :, h * dh:(h + 1) * dh]
            qb, kb, vb = qh.astype(BF16), kh.astype(BF16), vh.astype(BF16)
            bq = bcum[:, nh + h:nh + h + 1]
            br = bcum_t[nh + h:nh + h + 1, :]
            igr = gpre_t[h:h + 1, :]
            igc = gpre[:, h:h + 1]
            m_prev = m_s[b, h][:, 0:1]
            ct = ct_s[b, h]
            n_row = n_s[b, h]

            dmat = jnp.where(causal, (bq - br) + igr, NEG)
            a_inter = bq + m_prev
            m_t = jnp.maximum(a_inter, jnp.max(dmat, axis=-1, keepdims=True))
            w_intra = jnp.exp(dmat - m_t)
            sc = lax.dot_general(qb, kb, (((1,), (1,)), ((), ())), preferred_element_type=F32) * w_intra
            s_inter = jnp.exp(a_inter - m_t)
            num = (jnp.dot(sc.astype(BF16), vb, preferred_element_type=F32)
                   + s_inter * jnp.dot(qb, ct.astype(BF16), preferred_element_type=F32))
            den = (jnp.sum(sc, axis=-1, keepdims=True)
                   + s_inter * jnp.sum(qh * n_row, axis=-1, keepdims=True))
            hout = num / jnp.maximum(jnp.abs(den), jnp.exp(-m_t))
            og = _sigmoid(om_ref[b, :, h * dh:(h + 1) * dh])
            o_ref[b, :, h * dh:(h + 1) * dh] = (og * hout).astype(o_ref.dtype)

            gtot = br[:, lc - 1:lc]
            wst = (gtot - bq) + igc
            m_loc = jnp.max(wst, axis=0, keepdims=True)
            ek = jnp.exp(wst - m_loc) * kh
            c_loc = jnp.dot(ek.T.astype(BF16), vb, preferred_element_type=F32)
            n_loc = jnp.sum(ek, axis=0, keepdims=True)
            m_new = jnp.maximum(gtot + m_prev, m_loc)
            fa = jnp.exp(gtot + m_prev - m_new)
            fb = jnp.exp(m_loc - m_new)
            ct_s[b, h] = fa * ct + fb * c_loc
            n_s[b, h] = fa * n_row + fb * n_loc
            m_s[b, h] = jnp.broadcast_to(m_new, (1, LANES))


def _mlstm(proj3, cw, cb, gb, batch, seq):
    nc = seq // MLSTM_CHUNK
    wide = lambda off: pl.BlockSpec((batch, MLSTM_CHUNK, MLSTM_W), lambda c, off=off: (0, c, off // MLSTM_W))
    full = lambda shape: pl.BlockSpec(shape, lambda c: (0,) * len(shape))
    return pl.pallas_call(
        functools.partial(_mlstm_kernel, batch=batch),
        grid=(nc,),
        in_specs=[wide(OFF_QM), wide(OFF_KM), wide(OFF_VM), wide(OFF_OM),
                  pl.BlockSpec((batch, MLSTM_CHUNK, LANES), lambda c: (0, c, OFF_IF // LANES)),
                  full((SUBLANES, 2 * MLSTM_W)), full((1, 2 * MLSTM_W)), full((1, LANES))],
        out_specs=pl.BlockSpec((batch, MLSTM_CHUNK, MLSTM_W), lambda c: (0, c, 0)),
        out_shape=jax.ShapeDtypeStruct((batch, seq, MLSTM_W), BF16),
        scratch_shapes=[pltpu.VMEM((batch, MLSTM_CHUNK + SUBLANES, 2 * MLSTM_W), F32),
                        pltpu.VMEM((batch, MLSTM_HEADS, MLSTM_DH, MLSTM_DH), F32),
                        pltpu.VMEM((batch, MLSTM_HEADS, 1, MLSTM_DH), F32),
                        pltpu.VMEM((batch, MLSTM_HEADS, 1, LANES), F32)],
        compiler_params=_cparams(("arbitrary",)),
    )(proj3, proj3, proj3, proj3, proj3, cw, cb, gb)


def _merge_kernel(x_ref, gate_ref, ya_ref, yc_ref, ym_ref, wa_ref, wc_ref, wm_ref, wo_ref,
                  gf_ref, wr_ref, br_ref, xo_ref, h2_ref, lg_ref, *, tm):
    d = D_MODEL
    merged = (_sigmoid(gate_ref[:, 0:d]) * jnp.dot(ya_ref[...], wa_ref[...], preferred_element_type=F32)
              + _sigmoid(gate_ref[:, d:2 * d]) * jnp.dot(yc_ref[...], wc_ref[...], preferred_element_type=F32)
              + _sigmoid(gate_ref[:, 2 * d:3 * d]) * jnp.dot(ym_ref[...], wm_ref[...], preferred_element_type=F32))
    xn = x_ref[...] + jnp.dot(merged.astype(BF16), wo_ref[...], preferred_element_type=F32)
    xo_ref[...] = xn
    h2 = _rms(xn, gf_ref[...])
    lg_ref[...] = jnp.dot(h2, wr_ref[...], precision=HIGHEST, preferred_element_type=F32) + br_ref[...]
    for s in range(ROW_TILES):
        h2_ref[pl.ds(s, tm, stride=ROW_TILES), :] = h2[:, s * LANES:(s + 1) * LANES]


def _merge(x2, proj, y_att, y_conv, y_m, wa, wc, wm, wo, g_ffn, w_r, b_r, tm=256):
    t = x2.shape[0]
    rows = lambda n: pl.BlockSpec((tm, n), lambda i: (i, 0))
    full = lambda a: pl.BlockSpec(a.shape, lambda i: (0, 0))
    return pl.pallas_call(
        functools.partial(_merge_kernel, tm=tm),
        grid=(t // tm,),
        in_specs=[rows(D_MODEL), rows(3 * D_MODEL), rows(ATT_KV), rows(CONV_CH), rows(MLSTM_W),
                  full(wa), full(wc), full(wm), full(wo), full(g_ffn), full(w_r), full(b_r)],
        out_specs=[rows(D_MODEL), pl.BlockSpec((tm * ROW_TILES, LANES), lambda i: (i, 0)), rows(LANES)],
        out_shape=[jax.ShapeDtypeStruct((t, D_MODEL), F32),
                   jax.ShapeDtypeStruct((t * ROW_TILES, LANES), F32),
                   jax.ShapeDtypeStruct((t, LANES), F32)],
        compiler_params=_cparams(("arbitrary",)),
    )(x2, proj, y_att, y_conv, y_m, wa, wc, wm, wo, g_ffn, w_r, b_r)


ROUTER_TILE = 512


def _router_kernel(lg_ref, idx_ref, wt_ref, cnt_ref, run):
    n = ROUTER_TILE
    ng, ne = N_GROUPS_MOE, EXPERTS_PER_GROUP

    @pl.when(pl.program_id(0) == 0)
    def _():
        run[...] = jnp.zeros_like(run)

    lt = lg_ref[...].T
    lgrp = lt[0:ng, :]
    gmax = jnp.max(lgrp, axis=0, keepdims=True)
    grow = lax.broadcasted_iota(I32, (ng, n), 0)
    g_sel = jnp.min(jnp.where(lgrp == gmax, grow, ng), axis=0, keepdims=True)
    p_g = 1.0 / jnp.sum(jnp.exp(lgrp - gmax), axis=0, keepdims=True)

    le_sel = jnp.zeros((ne, n), F32)
    for g in range(ng):
        le_sel = jnp.where(g_sel == g, lt[(g + 1) * ne:(g + 2) * ne, :], le_sel)
    erow = lax.broadcasted_iota(I32, (ne, n), 0)
    v1 = jnp.max(le_sel, axis=0, keepdims=True)
    i1 = jnp.min(jnp.where(le_sel == v1, erow, ne), axis=0, keepdims=True)
    rest = jnp.where(erow == i1, -jnp.inf, le_sel)
    v2 = jnp.max(rest, axis=0, keepdims=True)
    i2 = jnp.min(jnp.where(rest == v2, erow, ne), axis=0, keepdims=True)
    e2 = jnp.exp(v2 - v1)
    w1 = p_g * (1.0 / (1.0 + e2))
    w2 = p_g * (e2 / (1.0 + e2))
    lo = jnp.minimum(i1, i2)
    hi = jnp.maximum(i1, i2)
    pair = ((lo * (2 * ne - 1 - lo)) >> 1) + (hi - lo - 1)
    cls = g_sel * PAIRS_PER_GROUP + pair
    first_is_lo = i1 < i2
    w_lo = jnp.where(first_is_lo, w1, w2)
    w_hi = jnp.where(first_is_lo, w2, w1)

    hit = lax.broadcasted_iota(I32, (LANES, n), 0) == cls
    onehot = jnp.where(hit, 1.0, 0.0)
    r_i = lax.broadcasted_iota(I32, (n, n), 0)
    c_i = lax.broadcasted_iota(I32, (n, n), 1)
    before = jnp.where(r_i < c_i, 1.0, 0.0).astype(BF16)
    cnt = jnp.dot(onehot.astype(BF16), before, preferred_element_type=F32) + run[:, 0:1]
    rank = jnp.sum(jnp.where(hit, cnt, 0.0), axis=0, keepdims=True)
    run[...] = run[...] + jnp.sum(onehot, axis=1, keepdims=True)

    idx_ref[...] = jnp.concatenate([cls, rank.astype(I32), jnp.zeros((SUBLANES - 2, n), I32)], axis=0)
    wt_ref[...] = jnp.concatenate([w_lo, w_hi, jnp.zeros((SUBLANES - 2, n), F32)], axis=0)
    cnt_ref[...] = run[...]


def _router(logits):
    t = logits.shape[0]
    n = ROUTER_TILE
    return pl.pallas_call(
        _router_kernel,
        grid=(t // n,),
        in_specs=[pl.BlockSpec((n, LANES), lambda i: (i, 0))],
        out_specs=[pl.BlockSpec((SUBLANES, n), lambda i: (0, i)),
                   pl.BlockSpec((SUBLANES, n), lambda i: (0, i)),
                   pl.BlockSpec((LANES, LANES), lambda i: (0, 0))],
        out_shape=[jax.ShapeDtypeStruct((SUBLANES, t), I32),
                   jax.ShapeDtypeStruct((SUBLANES, t), F32),
                   jax.ShapeDtypeStruct((LANES, LANES), F32)],
        scratch_shapes=[pltpu.VMEM((LANES, LANES), F32)],
        compiler_params=_cparams(("arbitrary",)),
    )(logits)


DISPATCH_TILE = 512


def _dispatch_kernel(dest_ref, x_ref, xs_in_ref, xs_ref, sem):
    del xs_in_ref
    n = DISPATCH_TILE

    def row_copy(t):
        d = pl.multiple_of(dest_ref[0, 0, t], ROW_TILES)
        src = x_ref.at[pl.ds(pl.multiple_of(t * ROW_TILES, ROW_TILES), ROW_TILES), :]
        return pltpu.make_async_copy(src, xs_ref.at[pl.ds(d, ROW_TILES), :], sem)

    def issue(t, carry):
        row_copy(t).start()
        return carry

    lax.fori_loop(0, n, issue, 0, unroll=8)

    def drain(t, carry):
        row_copy(t).wait()
        return carry

    lax.fori_loop(0, n, drain, 0, unroll=8)


def _dispatch(h2t, dest_tiles, p_len):
    t = h2t.shape[0] // ROW_TILES
    n = DISPATCH_TILE
    xs0 = jnp.zeros((p_len * ROW_TILES, LANES), F32)
    return pl.pallas_call(
        _dispatch_kernel,
        grid=(t // n,),
        in_specs=[pl.BlockSpec((1, 1, n), lambda i: (i, 0, 0), memory_space=pltpu.SMEM),
                  pl.BlockSpec((n * ROW_TILES, LANES), lambda i: (i, 0)),
                  pl.BlockSpec(memory_space=pl.ANY)],
        out_specs=pl.BlockSpec(memory_space=pl.ANY),
        out_shape=jax.ShapeDtypeStruct((p_len * ROW_TILES, LANES), F32),
        scratch_shapes=[pltpu.SemaphoreType.DMA(())],
        input_output_aliases={2: 0},
        compiler_params=_cparams(("arbitrary",)),
    )(dest_tiles, h2t, xs0)


def _expert_kernel(lo_ref, hi_ref, nu_ref, xs_ref, wg0, wu0, wd0, wg1, wu1, wd1, y_ref):
    del lo_ref, hi_ref

    @pl.when(pl.program_id(0) < nu_ref[0])
    def _():
        xb = jnp.concatenate(
            [xs_ref[pl.ds(s, MOE_BLOCK, stride=ROW_TILES), :] for s in range(ROW_TILES)], axis=-1).astype(BF16)
        for half, (wg, wu, wd) in enumerate(((wg0, wu0, wd0), (wg1, wu1, wd1))):
            hg = jnp.dot(xb, wg[...], preferred_element_type=F32)
            hu = jnp.dot(xb, wu[...], preferred_element_type=F32)
            hb = (hg * _sigmoid(hg)) * hu
            y = jnp.dot(hb.astype(BF16), wd[...], preferred_element_type=F32)
            for s in range(ROW_TILES):
                y_ref[pl.ds(half * ROW_TILES + s, MOE_BLOCK, stride=2 * ROW_TILES), :] = (
                    y[:, s * LANES:(s + 1) * LANES])

    @pl.when(pl.program_id(0) >= nu_ref[0])
    def _():
        y_ref[...] = jnp.zeros_like(y_ref)


def _experts(blk_lo, blk_hi, n_used, xs, wg, wu, wd):
    nb = blk_lo.shape[0]
    last = lambda i, nu: jnp.minimum(i, nu[0] - 1)
    xrows = pl.BlockSpec((MOE_BLOCK * ROW_TILES, LANES), lambda i, lo, hi, nu: (last(i, nu), 0))
    yrows = pl.BlockSpec((MOE_BLOCK * 2 * ROW_TILES, LANES), lambda i, lo, hi, nu: (i, 0))
    w_lo = lambda shape: pl.BlockSpec((None,) + shape, lambda i, lo, hi, nu: (lo[last(i, nu)], 0, 0))
    w_hi = lambda shape: pl.BlockSpec((None,) + shape, lambda i, lo, hi, nu: (hi[last(i, nu)], 0, 0))
    up, down = (D_MODEL, D_EXPERT), (D_EXPERT, D_MODEL)
    return pl.pallas_call(
        _expert_kernel,
        grid_spec=pltpu.PrefetchScalarGridSpec(
            num_scalar_prefetch=3,
            grid=(nb,),
            in_specs=[xrows, w_lo(up), w_lo(up), w_lo(down), w_hi(up), w_hi(up), w_hi(down)],
            out_specs=yrows),
        out_shape=jax.ShapeDtypeStruct((xs.shape[0] * 2, LANES), F32),
        compiler_params=_cparams(("arbitrary",)),
    )(blk_lo, blk_hi, n_used, xs, wg, wu, wd, wg, wu, wd)


COMBINE_TILE = 256
PAIR_TILES = 2 * ROW_TILES


def _combine_kernel(dcur_ref, dnxt_ref, x_ref, wt_ref, p_ref, y_hbm, gp_ref, wpg_ref, wpp_ref, gfin_ref,
                    o_ref, ybuf, sems, *, final):
    n = COMBINE_TILE
    i = pl.program_id(0)
    slot = i % 2

    def row_copy(dref, sl, t):
        d = pl.multiple_of(dref[0, 0, t], PAIR_TILES)
        dst = ybuf.at[sl, pl.ds(pl.multiple_of(t * PAIR_TILES, PAIR_TILES), PAIR_TILES), :]
        return pltpu.make_async_copy(y_hbm.at[pl.ds(d, PAIR_TILES), :], dst, sems.at[sl])

    def issue(dref, sl):
        def body(t, carry):
            row_copy(dref, sl, t).start()
            return carry
        lax.fori_loop(0, n, body, 0, unroll=8)

    @pl.when(i == 0)
    def _():
        issue(dcur_ref, slot)

    @pl.when(i + 1 < pl.num_programs(0))
    def _():
        issue(dnxt_ref, 1 - slot)

    pe = jnp.dot(p_ref[...].astype(BF16), wpp_ref[...], preferred_element_type=F32)

    def drain(t, carry):
        row_copy(dcur_ref, slot, t).wait()
        return carry

    lax.fori_loop(0, n, drain, 0, unroll=8)

    def rows_of(half):
        return jnp.concatenate(
            [ybuf[slot, pl.ds(half * ROW_TILES + s, n, stride=PAIR_TILES), :] for s in range(ROW_TILES)],
            axis=-1)

    xn = x_ref[...] + (wt_ref[:, 0:1] * rows_of(0) + wt_ref[:, 1:2] * rows_of(1))
    hp = _rms(xn, gp_ref[...])
    gate = _sigmoid(jnp.dot(hp.astype(BF16), wpg_ref[...], preferred_element_type=F32))
    xo = xn + gate * pe
    if final:
        xo = _rms(xo, gfin_ref[...])
    o_ref[...] = xo


def _combine(dest_tiles, x2, wts, p2, y, g_ple, wpg, wpp, g_final, final):
    t = x2.shape[0]
    n = COMBINE_TILE
    nt = t // n
    rows = lambda w: pl.BlockSpec((n, w), lambda i: (i, 0))
    full = lambda a: pl.BlockSpec(a.shape, lambda i: (0, 0))
    return pl.pallas_call(
        functools.partial(_combine_kernel, final=final),
        grid=(nt,),
        in_specs=[pl.BlockSpec((1, 1, n), lambda i: (i, 0, 0), memory_space=pltpu.SMEM),
                  pl.BlockSpec((1, 1, n), lambda i: (jnp.minimum(i + 1, nt - 1), 0, 0), memory_space=pltpu.SMEM),
                  rows(D_MODEL), rows(2), rows(PLE_DIM),
                  pl.BlockSpec(memory_space=pl.ANY),
                  full(g_ple), full(wpg), full(wpp), full(g_final)],
        out_specs=rows(D_MODEL),
        out_shape=jax.ShapeDtypeStruct((t, D_MODEL), F32),
        scratch_shapes=[pltpu.VMEM((2, n * PAIR_TILES, LANES), F32), pltpu.SemaphoreType.DMA((2,))],
        compiler_params=_cparams(("arbitrary",)),
    )(dest_tiles, dest_tiles, x2, wts, p2, y, g_ple, wpg, wpp, g_final)


def _class_experts():
    lo, hi = [], []
    for g in range(N_GROUPS_MOE):
        for a in range(EXPERTS_PER_GROUP):
            for b in range(a + 1, EXPERTS_PER_GROUP):
                lo.append(g * EXPERTS_PER_GROUP + a)
                hi.append(g * EXPERTS_PER_GROUP + b)
    return np.asarray(lo, np.int32), np.asarray(hi, np.int32)


def _moe_plan(idx, cnt, t):
    counts = cnt[:N_CLASSES, 0].astype(I32)
    padded = (counts + MOE_BLOCK - 1) // MOE_BLOCK * MOE_BLOCK
    ends_p = jnp.cumsum(padded)
    pstart = ends_p - padded
    classes = jnp.arange(N_CLASSES, dtype=I32)
    dest = jnp.sum(jnp.where(idx[0][:, None] == classes, pstart, 0), axis=-1) + idx[1]
    p_len = t + N_CLASSES * MOE_BLOCK
    nb = p_len // MOE_BLOCK
    blk_start = jnp.arange(nb, dtype=I32) * MOE_BLOCK
    blk_cls = jnp.minimum(jnp.sum((ends_p[None, :] <= blk_start[:, None]).astype(I32), axis=1), N_CLASSES - 1)
    lo_tab, hi_tab = _class_experts()
    is_c = blk_cls[:, None] == classes
    blk_lo = jnp.sum(jnp.where(is_c, lo_tab, 0), axis=1).astype(I32)
    blk_hi = jnp.sum(jnp.where(is_c, hi_tab, 0), axis=1).astype(I32)
    n_used = (ends_p[-1:] // MOE_BLOCK).astype(I32)
    return dest, blk_lo, blk_hi, n_used, p_len


def kernel(x, p, g_mix, w_in, conv_w, conv_b, conv_ln_g, conv_ln_b, mlstm_conv_w, mlstm_conv_b,
           mlstm_i_bias, mlstm_f_bias, w_br_att, w_br_conv, w_br_mlstm, w_out, g_ffn,
           w_router_g, b_router_g, w_router_e, b_router_e, w_exp_gate, w_exp_up, w_exp_down,
           g_ple, w_ple_gate, w_ple_proj, g_final):
    batch, seq, d = x.shape
    depth = w_in.shape[0]
    t = batch * seq
    bias_tab = jnp.asarray(_att_bias_table())
    x2 = x.reshape(t, d)
    nr = N_GROUPS_MOE + N_EXPERTS
    for i in range(depth):
        w = w_in[i]
        wp = jnp.concatenate([w[:, ORIG_SMALL + 2 * MLSTM_HEADS:], w[:, :ORIG_SMALL],
                              w[:, ORIG_SMALL:ORIG_SMALL + 2 * MLSTM_HEADS],
                              jnp.zeros((d, LANES - 2 * MLSTM_HEADS), F32)], axis=1).astype(BF16)
        proj = _inproj(x2, g_mix[i][None, :], wp)

        y_att = _attention(proj, bias_tab, batch, seq)
        cw = jnp.concatenate([conv_w[i], jnp.zeros((CONV_HALO - CONV_K, CONV_CH), F32)], axis=0)
        y_conv = _conv_module(proj, cw, conv_b[i][None, :], conv_ln_g[i][None, :], conv_ln_b[i][None, :],
                              batch, seq)
        mw = jnp.concatenate([mlstm_conv_w[i], jnp.zeros((SUBLANES - MLSTM_CONV_K, 2 * MLSTM_W), F32)], axis=0)
        gb = jnp.concatenate([mlstm_i_bias[i], mlstm_f_bias[i],
                              jnp.zeros((LANES - 2 * MLSTM_HEADS,), F32)])[None, :]
        y_m = _mlstm(proj.reshape(batch, seq, PROJ_N), mw, mlstm_conv_b[i][None, :], gb, batch, seq)

        gpad = EXPERTS_PER_GROUP - N_GROUPS_MOE
        w_r = jnp.concatenate([w_router_g[i], jnp.zeros((d, gpad), F32), w_router_e[i],
                               jnp.zeros((d, LANES - nr - gpad), F32)], axis=1)
        b_r = jnp.concatenate([b_router_g[i], jnp.zeros((gpad,), F32), b_router_e[i],
                               jnp.zeros((LANES - nr - gpad,), F32)])[None, :]
        x2, h2t, logits = _merge(x2, proj, y_att, y_conv, y_m.reshape(t, MLSTM_W),
                                 w_br_att[i].astype(BF16), w_br_conv[i].astype(BF16),
                                 w_br_mlstm[i].astype(BF16), w_out[i].astype(BF16),
                                 g_ffn[i][None, :], w_r, b_r)

        idx, wt, cnt = _router(logits)
        dest, blk_lo, blk_hi, n_used, p_len = _moe_plan(idx, cnt, t)
        xs = _dispatch(h2t, (dest * ROW_TILES).reshape(t // DISPATCH_TILE, 1, DISPATCH_TILE), p_len)
        ys = _experts(blk_lo, blk_hi, n_used, xs, w_exp_gate[i].astype(BF16), w_exp_up[i].astype(BF16),
                      w_exp_down[i].astype(BF16))
        x2 = _combine((dest * PAIR_TILES).reshape(t // COMBINE_TILE, 1, COMBINE_TILE), x2, wt[0:2].T,
                      p[i].reshape(t, PLE_DIM), ys, g_ple[i][None, :], w_ple_gate[i].astype(BF16),
                      w_ple_proj[i].astype(BF16), g_final[None, :], final=(i == depth - 1))
    return x2.reshape(batch, seq, d)
```

```python
import functools

import numpy as np
import jax
import jax.numpy as jnp
from jax import lax
from jax.experimental import pallas as pl
from jax.experimental.pallas import tpu as pltpu

F32 = jnp.float32
BF16 = jnp.bfloat16
I32 = jnp.int32
HIGHEST = lax.Precision.HIGHEST

D_MODEL = 1024
HEAD_DIM = 64
ATT_SLOTS = 8
DILATIONS = (1, 4, 16)
N_DIL = 3
ATT_BLOCK = 128
ATT_Q = ATT_SLOTS * N_DIL * HEAD_DIM
ATT_KV = ATT_SLOTS * HEAD_DIM
ALIBI_MAX_EXP = 8.0
CONV_CH = 512
CONV_K = 31
MLSTM_HEADS = 4
MLSTM_DH = 128
MLSTM_W = MLSTM_HEADS * MLSTM_DH
MLSTM_CONV_K = 4
MLSTM_CHUNK = 128
N_GROUPS_MOE = 4
EXPERTS_PER_GROUP = 8
N_EXPERTS = N_GROUPS_MOE * EXPERTS_PER_GROUP
D_EXPERT = 256
MOE_BLOCK = 128
PAIRS_PER_GROUP = EXPERTS_PER_GROUP * (EXPERTS_PER_GROUP - 1) // 2
N_CLASSES = N_GROUPS_MOE * PAIRS_PER_GROUP
PLE_DIM = 256
EPS = 1e-6
NEG = -1e30

LANES = 128
SUBLANES = 8
ROW_TILES = D_MODEL // LANES
PAIR_TILES = 2 * ROW_TILES
EXPERT_BLOCKS = 2
VMEM_LIMIT = 56 * 1024 * 1024

OFF_GATE = 0
OFF_QA = 3 * D_MODEL
OFF_KA = OFF_QA + ATT_Q
OFF_VA = OFF_KA + ATT_KV
OFF_CA = OFF_VA + ATT_KV
OFF_CG = OFF_CA + CONV_CH
OFF_QM = OFF_CG + CONV_CH
OFF_KM = OFF_QM + MLSTM_W
OFF_VM = OFF_KM + MLSTM_W
OFF_OM = OFF_VM + MLSTM_W
OFF_IF = OFF_OM + MLSTM_W
PROJ_N = OFF_IF + LANES
PROJ_TN = PROJ_N // 3
ORIG_SMALL = ATT_Q + 2 * ATT_KV + 2 * CONV_CH + 4 * MLSTM_W

ATT_TILE = ATT_BLOCK * DILATIONS[-1]
ATT_UNROLL = 16


def _cparams(sem, vmem=VMEM_LIMIT):
    return pltpu.CompilerParams(dimension_semantics=sem, vmem_limit_bytes=vmem)


def _rms(x, g):
    r = lax.rsqrt(jnp.mean(x * x, axis=-1, keepdims=True) + EPS)
    return (x * r) * g


def _sigmoid(x):
    return 1.0 / (1.0 + jnp.exp(-x))


def _log_sigmoid(x):
    return jnp.minimum(x, 0.0) - jnp.log(1.0 + jnp.exp(-jnp.abs(x)))


def _inproj_kernel(x_ref, g_ref, w_ref, o_ref):
    h = _rms(x_ref[...], g_ref[...]).astype(BF16)
    for j in range(PROJ_N // PROJ_TN):
        cols = slice(j * PROJ_TN, (j + 1) * PROJ_TN)
        o_ref[:, cols] = jnp.dot(h, w_ref[:, cols], preferred_element_type=F32)


def _inproj(x2, g, w, tm=256):
    t = x2.shape[0]
    return pl.pallas_call(
        _inproj_kernel,
        grid=(t // tm,),
        in_specs=[pl.BlockSpec((tm, D_MODEL), lambda i: (i, 0)),
                  pl.BlockSpec((1, D_MODEL), lambda i: (0, 0)),
                  pl.BlockSpec((D_MODEL, PROJ_N), lambda i: (0, 0), pipeline_mode=pl.Buffered(1))],
        out_specs=pl.BlockSpec((tm, PROJ_N), lambda i: (i, 0)),
        out_shape=jax.ShapeDtypeStruct((t, PROJ_N), F32),
        compiler_params=_cparams(("arbitrary",)),
    )(x2, g, w)


def _att_bias_table():
    j = np.arange(1, ATT_SLOTS * N_DIL + 1, dtype=np.float64)
    slopes = (2.0 ** (-ALIBI_MAX_EXP * j / (ATT_SLOTS * N_DIL))).reshape(N_DIL, ATT_SLOTS)
    qi = np.arange(ATT_BLOCK)[:, None] + ATT_BLOCK
    ki = np.arange(2 * ATT_BLOCK)[None, :]
    dist = qi - ki
    valid = (dist >= 0) & (dist <= ATT_BLOCK)
    tab = np.zeros((ATT_SLOTS // 2, N_DIL * 4, ATT_BLOCK, 2 * ATT_BLOCK), np.float32)
    for hp in range(ATT_SLOTS // 2):
        for g, dil in enumerate(DILATIONS):
            for h in range(2):
                slope = np.float32(slopes[g, 2 * hp + h])
                bias = -(slope * (dil * dist).astype(np.float32))
                tab[hp, g * 4 + h * 2 + 0] = np.where(valid, bias, NEG)
                tab[hp, g * 4 + h * 2 + 1] = np.where(valid & (ki >= ATT_BLOCK), bias, NEG)
    return tab


def _att_kernel(q0_ref, q1_ref, q2_ref, kc_ref, kp_ref, vc_ref, vp_ref, bias_ref, o_ref,
                qs, ks0, ks1, ks2, vs0, vs1, vs2, og, lg, onat, lnat, tmp):
    i = pl.program_id(2)
    q_refs = (q0_ref, q1_ref, q2_ref)
    k_scr = (ks0, ks1, ks2)
    v_scr = (vs0, vs1, vs2)
    scale = HEAD_DIM ** -0.5

    for g, dil in enumerate(DILATIONS[:2]):
        lt = ATT_TILE // dil
        cs = ATT_BLOCK + lt
        for r in range(dil):
            cur = pl.ds(r, lt, stride=dil) if dil > 1 else pl.ds(0, lt)
            prv = (pl.ds(ATT_TILE - ATT_BLOCK * dil + r, ATT_BLOCK, stride=dil) if dil > 1
                   else pl.ds(ATT_TILE - ATT_BLOCK, ATT_BLOCK))
            qs[g, r * lt:(r + 1) * lt, :] = (q_refs[g][cur, :] * scale).astype(BF16)
            k_scr[g][r * cs + ATT_BLOCK:(r + 1) * cs, :] = kc_ref[cur, :].astype(BF16)
            k_scr[g][r * cs:r * cs + ATT_BLOCK, :] = kp_ref[prv, :].astype(BF16)
            v_scr[g][r * cs + ATT_BLOCK:(r + 1) * cs, :] = vc_ref[cur, :].astype(BF16)
            v_scr[g][r * cs:r * cs + ATT_BLOCK, :] = vp_ref[prv, :].astype(BF16)

    sub, quarter = DILATIONS[1], ATT_TILE // DILATIONS[1]

    def classes16(src_ref, put, mul=None):
        for r4 in range(sub):
            tmp[r4 * quarter:(r4 + 1) * quarter, :] = src_ref[pl.ds(r4, quarter, stride=sub), :]
        for r in range(DILATIONS[2]):
            v = tmp[pl.ds((r % sub) * quarter + r // sub, ATT_BLOCK, stride=sub), :]
            put(r, (v if mul is None else v * mul).astype(BF16))

    cs16 = 2 * ATT_BLOCK

    def put_q(r, v):
        qs[2, r * ATT_BLOCK:(r + 1) * ATT_BLOCK, :] = v

    def put_at(scr, off):
        def put(r, v):
            scr[r * cs16 + off:r * cs16 + off + ATT_BLOCK, :] = v
        return put

    classes16(q2_ref, put_q, scale)
    classes16(kc_ref, put_at(ks2, ATT_BLOCK))
    classes16(kp_ref, put_at(ks2, 0))
    classes16(vc_ref, put_at(vs2, ATT_BLOCK))
    classes16(vp_ref, put_at(vs2, 0))

    lane = lax.broadcasted_iota(I32, (ATT_BLOCK, LANES), 1)
    head0 = lane < HEAD_DIM

    for g, dil in enumerate(DILATIONS):
        nj = ATT_TILE // dil // ATT_BLOCK

        def block(t, carry, g=g, nj=nj):
            r = t // nj
            j = t - r * nj
            qoff = pl.multiple_of(t * ATT_BLOCK, ATT_BLOCK)
            koff = pl.multiple_of((r * (nj + 1) + j) * ATT_BLOCK, ATT_BLOCK)
            first = jnp.logical_and(i == 0, j == 0).astype(I32)
            qb = qs[g, pl.ds(qoff, ATT_BLOCK), :]
            kb = k_scr[g][pl.ds(koff, 2 * ATT_BLOCK), :]
            vb = v_scr[g][pl.ds(koff, 2 * ATT_BLOCK), :]
            outs, lses = [], []
            for h in range(2):
                qh = jnp.where(head0 if h == 0 else jnp.logical_not(head0), qb, jnp.zeros_like(qb))
                s = lax.dot_general(qh, kb, (((1,), (1,)), ((), ())), preferred_element_type=F32)
                s = s + bias_ref[g * 4 + h * 2 + first]
                m = jnp.max(s, axis=-1, keepdims=True)
                p = jnp.exp(s - m)
                den = jnp.sum(p, axis=-1, keepdims=True)
                o = jnp.dot(p.astype(BF16), vb, preferred_element_type=F32) / den
                outs.append(o)
                lses.append(jnp.broadcast_to(m + jnp.log(den), (ATT_BLOCK, LANES)))
            og[g, pl.ds(qoff, ATT_BLOCK), :] = jnp.where(head0, outs[0], outs[1])
            lg[g, pl.ds(qoff, ATT_BLOCK), :] = jnp.where(head0, lses[0], lses[1])
            return carry

        lax.fori_loop(0, ATT_TILE // ATT_BLOCK, block, 0, unroll=ATT_UNROLL)

    for r in range(sub):
        onat[0, pl.ds(r, quarter, stride=sub), :] = og[1, r * quarter:(r + 1) * quarter, :]
        lnat[0, pl.ds(r, quarter, stride=sub), :] = lg[1, r * quarter:(r + 1) * quarter, :]
    for src, dst in ((og, onat), (lg, lnat)):
        for r in range(DILATIONS[2]):
            tmp[pl.ds((r % sub) * quarter + r // sub, ATT_BLOCK, stride=sub), :] = (
                src[2, r * ATT_BLOCK:(r + 1) * ATT_BLOCK, :])
        for r4 in range(sub):
            dst[1, pl.ds(r4, quarter, stride=sub), :] = tmp[r4 * quarter:(r4 + 1) * quarter, :]

    rc = 256

    def merge(c, carry):
        rows = pl.ds(pl.multiple_of(c * rc, rc), rc)
        l0, l1, l2 = lg[0, rows, :], lnat[0, rows, :], lnat[1, rows, :]
        mx = jnp.maximum(jnp.maximum(l0, l1), l2)
        w0, w1, w2 = jnp.exp(l0 - mx), jnp.exp(l1 - mx), jnp.exp(l2 - mx)
        tot = w0 + w1 + w2
        o = (w0 / tot) * og[0, rows, :] + (w1 / tot) * onat[0, rows, :] + (w2 / tot) * onat[1, rows, :]
        o_ref[rows, :] = o.astype(o_ref.dtype)
        return carry

    lax.fori_loop(0, ATT_TILE // rc, merge, 0)


def _attention(proj, bias_tab, batch, seq):
    t = proj.shape[0]
    nt = seq // ATT_TILE
    cb = lambda off: off // LANES

    def cur(col):
        return pl.BlockSpec((ATT_TILE, LANES), lambda hp, b, i, col=col: (b * nt + i, col + hp))

    def prev(col):
        return pl.BlockSpec((ATT_TILE, LANES),
                            lambda hp, b, i, col=col: (b * nt + jnp.maximum(i - 1, 0), col + hp))

    npairs = ATT_SLOTS // 2
    kv_rows = [ATT_TILE + ATT_BLOCK * d for d in DILATIONS]
    return pl.pallas_call(
        _att_kernel,
        grid=(npairs, batch, nt),
        in_specs=[cur(cb(OFF_QA)), cur(cb(OFF_QA) + npairs), cur(cb(OFF_QA) + 2 * npairs),
                  cur(cb(OFF_KA)), prev(cb(OFF_KA)), cur(cb(OFF_VA)), prev(cb(OFF_VA)),
                  pl.BlockSpec((None, N_DIL * 4, ATT_BLOCK, 2 * ATT_BLOCK), lambda hp, b, i: (hp, 0, 0, 0))],
        out_specs=pl.BlockSpec((ATT_TILE, LANES), lambda hp, b, i: (b * nt + i, hp)),
        out_shape=jax.ShapeDtypeStruct((t, ATT_KV), BF16),
        scratch_shapes=([pltpu.VMEM((N_DIL, ATT_TILE, LANES), BF16)]
                        + [pltpu.VMEM((n, LANES), BF16) for n in kv_rows]
                        + [pltpu.VMEM((n, LANES), BF16) for n in kv_rows]
                        + [pltpu.VMEM((N_DIL, ATT_TILE, LANES), F32)] * 2
                        + [pltpu.VMEM((N_DIL - 1, ATT_TILE, LANES), F32)] * 2
                        + [pltpu.VMEM((ATT_TILE, LANES), F32)]),
        compiler_params=_cparams(("arbitrary", "arbitrary", "arbitrary")),
    )(proj, proj, proj, proj, proj, proj, proj, bias_tab)


CONV_TILE = 256
CONV_HALO = 32
CONV_ROWS = 32


def _conv_kernel(a_ref, gt_ref, ha_ref, hg_ref, w_ref, b_ref, lng_ref, lnb_ref, o_ref, ybuf, win):
    i = pl.program_id(1)
    halo = ha_ref[...] * _sigmoid(hg_ref[...])
    ybuf[0:CONV_HALO, :] = jnp.where(i > 0, halo, jnp.zeros_like(halo))
    ybuf[CONV_HALO:, :] = a_ref[...] * _sigmoid(gt_ref[...])
    first = CONV_HALO - (CONV_K - 1)
    for c in range(CONV_TILE // CONV_ROWS):
        for j in range(SUBLANES):
            span = ((CONV_K - 1 - j) // SUBLANES) * SUBLANES + CONV_ROWS
            lo = c * CONV_ROWS + first + j
            win[j, 0:span, :] = ybuf[lo:lo + span, :]
        acc = jnp.broadcast_to(b_ref[...], (CONV_ROWS, CONV_CH))
        for k in range(CONV_K):
            j, a = k % SUBLANES, k // SUBLANES
            acc = acc + w_ref[k:k + 1, :] * win[j, a * SUBLANES:a * SUBLANES + CONV_ROWS, :]
        mu = jnp.mean(acc, axis=-1, keepdims=True)
        cen = acc - mu
        var = jnp.mean(cen * cen, axis=-1, keepdims=True)
        y = (cen * lax.rsqrt(var + EPS)) * lng_ref[...] + lnb_ref[...]
        o_ref[c * CONV_ROWS:(c + 1) * CONV_ROWS, :] = (y * _sigmoid(y)).astype(o_ref.dtype)


def _conv_module(proj, w, b, ln_g, ln_b, batch, seq):
    t = proj.shape[0]
    nt = seq // CONV_TILE
    hb = CONV_TILE // CONV_HALO
    ca, cg = OFF_CA // CONV_CH, OFF_CG // CONV_CH
    cur = lambda col: pl.BlockSpec((CONV_TILE, CONV_CH), lambda bb, i, col=col: (bb * nt + i, col))
    halo = lambda col: pl.BlockSpec(
        (CONV_HALO, CONV_CH), lambda bb, i, col=col: (jnp.maximum((bb * nt + i) * hb - 1, 0), col))
    vec = pl.BlockSpec((1, CONV_CH), lambda bb, i: (0, 0))
    return pl.pallas_call(
        _conv_kernel,
        grid=(batch, nt),
        in_specs=[cur(ca), cur(cg), halo(ca), halo(cg),
                  pl.BlockSpec((CONV_HALO, CONV_CH), lambda bb, i: (0, 0)), vec, vec, vec],
        out_specs=pl.BlockSpec((CONV_TILE, CONV_CH), lambda bb, i: (bb * nt + i, 0)),
        out_shape=jax.ShapeDtypeStruct((t, CONV_CH), BF16),
        scratch_shapes=[pltpu.VMEM((CONV_HALO + CONV_TILE, CONV_CH), F32),
                        pltpu.VMEM((SUBLANES, CONV_HALO + CONV_ROWS, CONV_CH), F32)],
        compiler_params=_cparams(("arbitrary", "arbitrary")),
    )(proj, proj, proj, proj, w, b, ln_g, ln_b)


def _mlstm_kernel(q_ref, k_ref, v_ref, om_ref, gate_ref, cw_ref, cb_ref, gb_ref, o_ref,
                  ext, ct_s, n_s, m_s, *, batch):
    c = pl.program_id(0)
    lc, dh, nh = MLSTM_CHUNK, MLSTM_DH, MLSTM_HEADS
    pad = SUBLANES

    @pl.when(c == 0)
    def _():
        ext[:, 0:pad, :] = jnp.zeros((batch, pad, 2 * MLSTM_W), F32)
        ct_s[...] = jnp.zeros_like(ct_s)
        n_s[...] = jnp.zeros_like(n_s)
        m_s[...] = jnp.zeros_like(m_s)

    @pl.when(c > 0)
    def _():
        ext[:, 0:pad, :] = ext[:, lc:lc + pad, :]

    ext[:, pad:, 0:MLSTM_W] = q_ref[...]
    ext[:, pad:, MLSTM_W:] = k_ref[...]

    row = lax.broadcasted_iota(I32, (lc, lc), 0)
    col = lax.broadcasted_iota(I32, (lc, lc), 1)
    causal = row >= col
    tril = causal.astype(F32)
    is_f = jnp.logical_and(col >= nh, col < 2 * nh)

    for b in range(batch):
        acc = jnp.broadcast_to(cb_ref[...], (lc, 2 * MLSTM_W))
        for k in range(MLSTM_CONV_K):
            lo = pad - (MLSTM_CONV_K - 1) + k
            acc = acc + cw_ref[k:k + 1, :] * ext[b, lo:lo + lc, :]
        qk = acc * _sigmoid(acc)

        gpre = gate_ref[b] + gb_ref[...]
        lf = jnp.where(is_f, _log_sigmoid(gpre), jnp.zeros_like(gpre))
        bcum = jnp.dot(tril, lf, precision=HIGHEST, preferred_element_type=F32)
        gpre_t = gpre.T
        bcum_t = bcum.T

        for h in range(nh):
            qh = qk[:, h * dh:(h + 1) * dh]
            kh = qk[:, MLSTM_W + h * dh:MLSTM_W + (h + 1) * dh] * (dh ** -0.5)
            vh = v_ref[b, :, h * dh:(h + 1) * dh]
            qb, kb, vb = qh.astype(BF16), kh.astype(BF16), vh.astype(BF16)
            bq = bcum[:, nh + h:nh + h + 1]
            br = bcum_t[nh + h:nh + h + 1, :]
            igr = gpre_t[h:h + 1, :]
            igc = gpre[:, h:h + 1]
            m_prev = m_s[b, h][:, 0:1]
            ct = ct_s[b, h]
            n_row = n_s[b, h]

            dmat = jnp.where(causal, (bq - br) + igr, NEG)
            a_inter = bq + m_prev
            m_t = jnp.maximum(a_inter, jnp.max(dmat, axis=-1, keepdims=True))
            w_intra = jnp.exp(dmat - m_t)
            sc = lax.dot_general(qb, kb, (((1,), (1,)), ((), ())), preferred_element_type=F32) * w_intra
            s_inter = jnp.exp(a_inter - m_t)
            num = (jnp.dot(sc.astype(BF16), vb, preferred_element_type=F32)
                   + s_inter * jnp.dot(qb, ct.astype(BF16), preferred_element_type=F32))
            den = (jnp.sum(sc, axis=-1, keepdims=True)
                   + s_inter * jnp.sum(qh * n_row, axis=-1, keepdims=True))
            hout = num / jnp.maximum(jnp.abs(den), jnp.exp(-m_t))
            og = _sigmoid(om_ref[b, :, h * dh:(h + 1) * dh])
            o_ref[b, :, h * dh:(h + 1) * dh] = (og * hout).astype(o_ref.dtype)

            gtot = br[:, lc - 1:lc]
            wst = (gtot - bq) + igc
            m_loc = jnp.max(wst, axis=0, keepdims=True)
            ek = jnp.exp(wst - m_loc) * kh
            c_loc = jnp.dot(ek.T.astype(BF16), vb, preferred_element_type=F32)
            n_loc = jnp.sum(ek, axis=0, keepdims=True)
            m_new = jnp.maximum(gtot + m_prev, m_loc)
            fa = jnp.exp(gtot + m_prev - m_new)
            fb = jnp.exp(m_loc - m_new)
            ct_s[b, h] = fa * ct + fb * c_loc
            n_s[b, h] = fa * n_row + fb * n_loc
            m_s[b, h] = jnp.broadcast_to(m_new, (1, LANES))


def _mlstm(proj3, cw, cb, gb, batch, seq):
    nc = seq // MLSTM_CHUNK
    wide = lambda off: pl.BlockSpec((batch, MLSTM_CHUNK, MLSTM_W), lambda c, off=off: (0, c, off // MLSTM_W))
    full = lambda shape: pl.BlockSpec(shape, lambda c: (0,) * len(shape))
    return pl.pallas_call(
        functools.partial(_mlstm_kernel, batch=batch),
        grid=(nc,),
        in_specs=[wide(OFF_QM), wide(OFF_KM), wide(OFF_VM), wide(OFF_OM),
                  pl.BlockSpec((batch, MLSTM_CHUNK, LANES), lambda c: (0, c, OFF_IF // LANES)),
                  full((SUBLANES, 2 * MLSTM_W)), full((1, 2 * MLSTM_W)), full((1, LANES))],
        out_specs=pl.BlockSpec((batch, MLSTM_CHUNK, MLSTM_W), lambda c: (0, c, 0)),
        out_shape=jax.ShapeDtypeStruct((batch, seq, MLSTM_W), BF16),
        scratch_shapes=[pltpu.VMEM((batch, MLSTM_CHUNK + SUBLANES, 2 * MLSTM_W), F32),
                        pltpu.VMEM((batch, MLSTM_HEADS, MLSTM_DH, MLSTM_DH), F32),
                        pltpu.VMEM((batch, MLSTM_HEADS, 1, MLSTM_DH), F32),
                        pltpu.VMEM((batch, MLSTM_HEADS, 1, LANES), F32)],
        compiler_params=_cparams(("arbitrary",)),
    )(proj3, proj3, proj3, proj3, proj3, cw, cb, gb)


MERGE_ROWS = 256


def _merge_kernel(x_ref, gate_ref, ya_ref, yc_ref, ym_ref, wa_ref, wc_ref, wm_ref, wo_ref,
                  gf_ref, wr_ref, br_ref, xo_ref, h2_ref, lg_ref, *, tm):
    d = D_MODEL
    for r0 in range(0, tm, MERGE_ROWS):
        rows = slice(r0, r0 + MERGE_ROWS)
        branch = lambda y_ref, w_ref: jnp.dot(y_ref[rows, :], w_ref[...], preferred_element_type=F32)
        merged = (_sigmoid(gate_ref[rows, 0:d]) * branch(ya_ref, wa_ref)
                  + _sigmoid(gate_ref[rows, d:2 * d]) * branch(yc_ref, wc_ref)
                  + _sigmoid(gate_ref[rows, 2 * d:3 * d]) * branch(ym_ref, wm_ref))
        xn = x_ref[rows, :] + jnp.dot(merged.astype(BF16), wo_ref[...], preferred_element_type=F32)
        xo_ref[rows, :] = xn
        h2 = _rms(xn, gf_ref[...])
        lg_ref[rows, :] = jnp.dot(h2.astype(BF16), wr_ref[...], preferred_element_type=F32) + br_ref[...]
        for s in range(ROW_TILES):
            h2_ref[pl.ds(r0 * ROW_TILES + s, MERGE_ROWS, stride=ROW_TILES), :] = h2[:, s * LANES:(s + 1) * LANES]


def _merge(x2, proj, y_att, y_conv, y_m, wa, wc, wm, wo, g_ffn, w_r, b_r, tm=2 * MERGE_ROWS):
    t = x2.shape[0]
    rows = lambda n: pl.BlockSpec((tm, n), lambda i: (i, 0))
    full = lambda a: pl.BlockSpec(a.shape, lambda i: (0, 0))
    return pl.pallas_call(
        functools.partial(_merge_kernel, tm=tm),
        grid=(t // tm,),
        in_specs=[rows(D_MODEL), rows(3 * D_MODEL), rows(ATT_KV), rows(CONV_CH), rows(MLSTM_W),
                  full(wa), full(wc), full(wm), full(wo), full(g_ffn), full(w_r), full(b_r)],
        out_specs=[rows(D_MODEL), pl.BlockSpec((tm * ROW_TILES, LANES), lambda i: (i, 0)), rows(LANES)],
        out_shape=[jax.ShapeDtypeStruct((t, D_MODEL), F32),
                   jax.ShapeDtypeStruct((t * ROW_TILES, LANES), F32),
                   jax.ShapeDtypeStruct((t, LANES), F32)],
        compiler_params=_cparams(("arbitrary",)),
    )(x2, proj, y_att, y_conv, y_m, wa, wc, wm, wo, g_ffn, w_r, b_r)


ROUTER_TILE = 512


def _router_kernel(lg_ref, idx_ref, wt_ref, cnt_ref, run):
    n = ROUTER_TILE
    ng, ne = N_GROUPS_MOE, EXPERTS_PER_GROUP

    @pl.when(pl.program_id(0) == 0)
    def _():
        run[...] = jnp.zeros_like(run)

    lt = lg_ref[...].T
    lgrp = lt[0:ng, :]
    gmax = jnp.max(lgrp, axis=0, keepdims=True)
    grow = lax.broadcasted_iota(I32, (ng, n), 0)
    g_sel = jnp.min(jnp.where(lgrp == gmax, grow, ng), axis=0, keepdims=True)
    p_g = 1.0 / jnp.sum(jnp.exp(lgrp - gmax), axis=0, keepdims=True)

    le_sel = jnp.zeros((ne, n), F32)
    for g in range(ng):
        le_sel = jnp.where(g_sel == g, lt[(g + 1) * ne:(g + 2) * ne, :], le_sel)
    erow = lax.broadcasted_iota(I32, (ne, n), 0)
    v1 = jnp.max(le_sel, axis=0, keepdims=True)
    i1 = jnp.min(jnp.where(le_sel == v1, erow, ne), axis=0, keepdims=True)
    rest = jnp.where(erow == i1, -jnp.inf, le_sel)
    v2 = jnp.max(rest, axis=0, keepdims=True)
    i2 = jnp.min(jnp.where(rest == v2, erow, ne), axis=0, keepdims=True)
    e2 = jnp.exp(v2 - v1)
    w1 = p_g * (1.0 / (1.0 + e2))
    w2 = p_g * (e2 / (1.0 + e2))
    lo = jnp.minimum(i1, i2)
    hi = jnp.maximum(i1, i2)
    pair = ((lo * (2 * ne - 1 - lo)) >> 1) + (hi - lo - 1)
    cls = g_sel * PAIRS_PER_GROUP + pair
    first_is_lo = i1 < i2
    w_lo = jnp.where(first_is_lo, w1, w2)
    w_hi = jnp.where(first_is_lo, w2, w1)

    hit = lax.broadcasted_iota(I32, (LANES, n), 0) == cls
    onehot = jnp.where(hit, 1.0, 0.0)
    r_i = lax.broadcasted_iota(I32, (n, n), 0)
    c_i = lax.broadcasted_iota(I32, (n, n), 1)
    before = jnp.where(r_i < c_i, 1.0, 0.0).astype(BF16)
    cnt = jnp.dot(onehot.astype(BF16), before, preferred_element_type=F32) + run[:, 0:1]
    rank = jnp.sum(jnp.where(hit, cnt, 0.0), axis=0, keepdims=True)
    run[...] = run[...] + jnp.sum(onehot, axis=1, keepdims=True)

    idx_ref[...] = jnp.concatenate([cls, rank.astype(I32), jnp.zeros((SUBLANES - 2, n), I32)], axis=0)
    wt_ref[...] = jnp.concatenate([w_lo, w_hi, jnp.zeros((SUBLANES - 2, n), F32)], axis=0)
    cnt_ref[...] = run[...]


def _router(logits):
    t = logits.shape[0]
    n = ROUTER_TILE
    return pl.pallas_call(
        _router_kernel,
        grid=(t // n,),
        in_specs=[pl.BlockSpec((n, LANES), lambda i: (i, 0))],
        out_specs=[pl.BlockSpec((SUBLANES, n), lambda i: (0, i)),
                   pl.BlockSpec((SUBLANES, n), lambda i: (0, i)),
                   pl.BlockSpec((LANES, LANES), lambda i: (0, 0))],
        out_shape=[jax.ShapeDtypeStruct((SUBLANES, t), I32),
                   jax.ShapeDtypeStruct((SUBLANES, t), F32),
                   jax.ShapeDtypeStruct((LANES, LANES), F32)],
        scratch_shapes=[pltpu.VMEM((LANES, LANES), F32)],
        compiler_params=_cparams(("arbitrary",)),
    )(logits)


DISPATCH_TILE = 512


def _dispatch_kernel(dest_ref, x_ref, xs_in_ref, xs_ref, sem):
    del xs_in_ref
    n = DISPATCH_TILE

    def row_copy(t):
        d = pl.multiple_of(dest_ref[0, 0, t], ROW_TILES)
        src = x_ref.at[pl.ds(pl.multiple_of(t * ROW_TILES, ROW_TILES), ROW_TILES), :]
        return pltpu.make_async_copy(src, xs_ref.at[pl.ds(d, ROW_TILES), :], sem)

    def issue(t, carry):
        row_copy(t).start()
        return carry

    lax.fori_loop(0, n, issue, 0, unroll=8)

    def drain(t, carry):
        row_copy(t).wait()
        return carry

    lax.fori_loop(0, n, drain, 0, unroll=8)


def _dispatch(h2t, dest_tiles, p_len):
    t = h2t.shape[0] // ROW_TILES
    n = DISPATCH_TILE
    xs0 = jnp.zeros((p_len * ROW_TILES, LANES), F32)
    return pl.pallas_call(
        _dispatch_kernel,
        grid=(t // n,),
        in_specs=[pl.BlockSpec((1, 1, n), lambda i: (i, 0, 0), memory_space=pltpu.SMEM),
                  pl.BlockSpec((n * ROW_TILES, LANES), lambda i: (i, 0)),
                  pl.BlockSpec(memory_space=pl.ANY)],
        out_specs=pl.BlockSpec(memory_space=pl.ANY),
        out_shape=jax.ShapeDtypeStruct((p_len * ROW_TILES, LANES), F32),
        scratch_shapes=[pltpu.SemaphoreType.DMA(())],
        input_output_aliases={2: 0},
        compiler_params=_cparams(("arbitrary",)),
    )(dest_tiles, h2t, xs0)


def _expert_kernel(lo_ref, hi_ref, nu_ref, xs_ref, *refs):
    del lo_ref, hi_ref
    w_refs, y_ref = refs[:-1], refs[-1]
    first = pl.program_id(0) * EXPERT_BLOCKS
    used = nu_ref[0] - first

    def run(sub):
        rows_in = sub * MOE_BLOCK * ROW_TILES
        rows_out = sub * MOE_BLOCK * PAIR_TILES
        xb = jnp.concatenate(
            [xs_ref[pl.ds(rows_in + s, MOE_BLOCK, stride=ROW_TILES), :] for s in range(ROW_TILES)],
            axis=-1).astype(BF16)
        for half in range(2):
            wg, wu, wd = w_refs[sub * 6 + half * 3:sub * 6 + half * 3 + 3]
            hg = jnp.dot(xb, wg[...], preferred_element_type=F32)
            hu = jnp.dot(xb, wu[...], preferred_element_type=F32)
            hb = (hg * _sigmoid(hg)) * hu
            y = jnp.dot(hb.astype(BF16), wd[...], preferred_element_type=F32)
            for s in range(ROW_TILES):
                y_ref[pl.ds(rows_out + half * ROW_TILES + s, MOE_BLOCK, stride=PAIR_TILES), :] = (
                    y[:, s * LANES:(s + 1) * LANES])

    def clear(sub):
        n = MOE_BLOCK * PAIR_TILES
        y_ref[sub * n:(sub + 1) * n, :] = jnp.zeros((n, LANES), F32)

    for count in range(EXPERT_BLOCKS + 1):
        @pl.when(jnp.clip(used, 0, EXPERT_BLOCKS) == count)
        def _(count=count):
            for sub in range(EXPERT_BLOCKS):
                (run if sub < count else clear)(sub)


def _experts(blk_lo, blk_hi, n_used, xs, wg, wu, wd):
    nb = blk_lo.shape[0]
    last = lambda b, nu: jnp.minimum(b, nu[0] - 1)
    xrows = pl.BlockSpec((EXPERT_BLOCKS * MOE_BLOCK * ROW_TILES, LANES), lambda i, lo, hi, nu: (i, 0))
    yrows = pl.BlockSpec((EXPERT_BLOCKS * MOE_BLOCK * PAIR_TILES, LANES), lambda i, lo, hi, nu: (i, 0))
    up, down = (D_MODEL, D_EXPERT), (D_EXPERT, D_MODEL)

    def weight(shape, sub, high):
        def index(i, lo, hi, nu):
            b = last(i * EXPERT_BLOCKS + sub, nu)
            return ((hi if high else lo)[b], 0, 0)
        return pl.BlockSpec((None,) + shape, index)

    w_specs, w_args = [], []
    for sub in range(EXPERT_BLOCKS):
        for high in (False, True):
            w_specs += [weight(up, sub, high), weight(up, sub, high), weight(down, sub, high)]
            w_args += [wg, wu, wd]
    return pl.pallas_call(
        _expert_kernel,
        grid_spec=pltpu.PrefetchScalarGridSpec(
            num_scalar_prefetch=3,
            grid=(nb // EXPERT_BLOCKS,),
            in_specs=[xrows] + w_specs,
            out_specs=yrows),
        out_shape=jax.ShapeDtypeStruct((xs.shape[0] * 2, LANES), F32),
        compiler_params=_cparams(("arbitrary",)),
    )(blk_lo, blk_hi, n_used, xs, *w_args)


COMBINE_TILE = 256


def _combine_kernel(dcur_ref, dnxt_ref, x_ref, wt_ref, p_ref, y_hbm, gp_ref, wpg_ref, wpp_ref, gfin_ref,
                    o_ref, ybuf, sems, *, final):
    n = COMBINE_TILE
    i = pl.program_id(0)
    slot = i % 2

    def row_copy(dref, sl, t):
        d = pl.multiple_of(dref[0, 0, t], PAIR_TILES)
        dst = ybuf.at[sl, pl.ds(pl.multiple_of(t * PAIR_TILES, PAIR_TILES), PAIR_TILES), :]
        return pltpu.make_async_copy(y_hbm.at[pl.ds(d, PAIR_TILES), :], dst, sems.at[sl])

    def issue(dref, sl):
        def body(t, carry):
            row_copy(dref, sl, t).start()
            return carry
        lax.fori_loop(0, n, body, 0, unroll=8)

    @pl.when(i == 0)
    def _():
        issue(dcur_ref, slot)

    @pl.when(i + 1 < pl.num_programs(0))
    def _():
        issue(dnxt_ref, 1 - slot)

    pe = jnp.dot(p_ref[...].astype(BF16), wpp_ref[...], preferred_element_type=F32)

    def drain(t, carry):
        row_copy(dcur_ref, slot, t).wait()
        return carry

    lax.fori_loop(0, n, drain, 0, unroll=8)

    def rows_of(half):
        return jnp.concatenate(
            [ybuf[slot, pl.ds(half * ROW_TILES + s, n, stride=PAIR_TILES), :] for s in range(ROW_TILES)],
            axis=-1)

    xn = x_ref[...] + (wt_ref[:, 0:1] * rows_of(0) + wt_ref[:, 1:2] * rows_of(1))
    hp = _rms(xn, gp_ref[...])
    gate = _sigmoid(jnp.dot(hp.astype(BF16), wpg_ref[...], preferred_element_type=F32))
    xo = xn + gate * pe
    if final:
        xo = _rms(xo, gfin_ref[...])
    o_ref[...] = xo


def _combine(dest_tiles, x2, wts, p2, y, g_ple, wpg, wpp, g_final, final):
    t = x2.shape[0]
    n = COMBINE_TILE
    nt = t // n
    rows = lambda w: pl.BlockSpec((n, w), lambda i: (i, 0))
    full = lambda a: pl.BlockSpec(a.shape, lambda i: (0, 0))
    return pl.pallas_call(
        functools.partial(_combine_kernel, final=final),
        grid=(nt,),
        in_specs=[pl.BlockSpec((1, 1, n), lambda i: (i, 0, 0), memory_space=pltpu.SMEM),
                  pl.BlockSpec((1, 1, n), lambda i: (jnp.minimum(i + 1, nt - 1), 0, 0), memory_space=pltpu.SMEM),
                  rows(D_MODEL), rows(2), rows(PLE_DIM),
                  pl.BlockSpec(memory_space=pl.ANY),
                  full(g_ple), full(wpg), full(wpp), full(g_final)],
        out_specs=rows(D_MODEL),
        out_shape=jax.ShapeDtypeStruct((t, D_MODEL), F32),
        scratch_shapes=[pltpu.VMEM((2, n * PAIR_TILES, LANES), F32), pltpu.SemaphoreType.DMA((2,))],
        compiler_params=_cparams(("arbitrary",)),
    )(dest_tiles, dest_tiles, x2, wts, p2, y, g_ple, wpg, wpp, g_final)


def _class_experts():
    lo, hi = [], []
    for g in range(N_GROUPS_MOE):
        for a in range(EXPERTS_PER_GROUP):
            for b in range(a + 1, EXPERTS_PER_GROUP):
                lo.append(g * EXPERTS_PER_GROUP + a)
                hi.append(g * EXPERTS_PER_GROUP + b)
    return np.asarray(lo, np.int32), np.asarray(hi, np.int32)


def _moe_plan(idx, cnt, t):
    counts = cnt[:N_CLASSES, 0].astype(I32)
    padded = (counts + MOE_BLOCK - 1) // MOE_BLOCK * MOE_BLOCK
    ends_p = jnp.cumsum(padded)
    pstart = ends_p - padded
    classes = jnp.arange(N_CLASSES, dtype=I32)
    dest = jnp.sum(jnp.where(idx[0][:, None] == classes, pstart, 0), axis=-1) + idx[1]
    p_len = t + N_CLASSES * MOE_BLOCK
    nb = p_len // MOE_BLOCK
    blk_start = jnp.arange(nb, dtype=I32) * MOE_BLOCK
    blk_cls = jnp.minimum(jnp.sum((ends_p[None, :] <= blk_start[:, None]).astype(I32), axis=1), N_CLASSES - 1)
    lo_tab, hi_tab = _class_experts()
    is_c = blk_cls[:, None] == classes
    blk_lo = jnp.sum(jnp.where(is_c, lo_tab, 0), axis=1).astype(I32)
    blk_hi = jnp.sum(jnp.where(is_c, hi_tab, 0), axis=1).astype(I32)
    n_used = (ends_p[-1:] // MOE_BLOCK).astype(I32)
    return dest, blk_lo, blk_hi, n_used, p_len


def kernel(x, p, g_mix, w_in, conv_w, conv_b, conv_ln_g, conv_ln_b, mlstm_conv_w, mlstm_conv_b,
           mlstm_i_bias, mlstm_f_bias, w_br_att, w_br_conv, w_br_mlstm, w_out, g_ffn,
           w_router_g, b_router_g, w_router_e, b_router_e, w_exp_gate, w_exp_up, w_exp_down,
           g_ple, w_ple_gate, w_ple_proj, g_final):
    batch, seq, d = x.shape
    depth = w_in.shape[0]
    t = batch * seq
    bias_tab = jnp.asarray(_att_bias_table())
    x2 = x.reshape(t, d)
    nr = N_GROUPS_MOE + N_EXPERTS
    for i in range(depth):
        w = w_in[i]
        wp = jnp.concatenate([w[:, ORIG_SMALL + 2 * MLSTM_HEADS:], w[:, :ORIG_SMALL],
                              w[:, ORIG_SMALL:ORIG_SMALL + 2 * MLSTM_HEADS],
                              jnp.zeros((d, LANES - 2 * MLSTM_HEADS), F32)], axis=1).astype(BF16)
        proj = _inproj(x2, g_mix[i][None, :], wp)

        y_att = _attention(proj, bias_tab, batch, seq)
        cw = jnp.concatenate([conv_w[i], jnp.zeros((CONV_HALO - CONV_K, CONV_CH), F32)], axis=0)
        y_conv = _conv_module(proj, cw, conv_b[i][None, :], conv_ln_g[i][None, :], conv_ln_b[i][None, :],
                              batch, seq)
        mw = jnp.concatenate([mlstm_conv_w[i], jnp.zeros((SUBLANES - MLSTM_CONV_K, 2 * MLSTM_W), F32)], axis=0)
        gb = jnp.concatenate([mlstm_i_bias[i], mlstm_f_bias[i],
                              jnp.zeros((LANES - 2 * MLSTM_HEADS,), F32)])[None, :]
        y_m = _mlstm(proj.reshape(batch, seq, PROJ_N), mw, mlstm_conv_b[i][None, :], gb, batch, seq)

        gpad = EXPERTS_PER_GROUP - N_GROUPS_MOE
        w_r = jnp.concatenate([w_router_g[i], jnp.zeros((d, gpad), F32), w_router_e[i],
                               jnp.zeros((d, LANES - nr - gpad), F32)], axis=1)
        b_r = jnp.concatenate([b_router_g[i], jnp.zeros((gpad,), F32), b_router_e[i],
                               jnp.zeros((LANES - nr - gpad,), F32)])[None, :]
        x2, h2t, logits = _merge(x2, proj, y_att, y_conv, y_m.reshape(t, MLSTM_W),
                                 w_br_att[i].astype(BF16), w_br_conv[i].astype(BF16),
                                 w_br_mlstm[i].astype(BF16), w_out[i].astype(BF16),
                                 g_ffn[i][None, :], w_r.astype(BF16), b_r)

        idx, wt, cnt = _router(logits)
        dest, blk_lo, blk_hi, n_used, p_len = _moe_plan(idx, cnt, t)
        xs = _dispatch(h2t, (dest * ROW_TILES).reshape(t // DISPATCH_TILE, 1, DISPATCH_TILE), p_len)
        ys = _experts(blk_lo, blk_hi, n_used, xs, w_exp_gate[i].astype(BF16), w_exp_up[i].astype(BF16),
                      w_exp_down[i].astype(BF16))
        x2 = _combine((dest * PAIR_TILES).reshape(t // COMBINE_TILE, 1, COMBINE_TILE), x2, wt[0:2].T,
                      p[i].reshape(t, PLE_DIM), ys, g_ple[i][None, :], w_ple_gate[i].astype(BF16),
                      w_ple_proj[i].astype(BF16), g_final[None, :], final=(i == depth - 1))
    return x2.reshape(batch, seq, d)
```

```python
import functools

import numpy as np
import jax
import jax.numpy as jnp
from jax import lax
from jax.experimental import pallas as pl
from jax.experimental.pallas import tpu as pltpu

F32 = jnp.float32
BF16 = jnp.bfloat16
I32 = jnp.int32
HIGHEST = lax.Precision.HIGHEST

D_MODEL = 1024
HEAD_DIM = 64
ATT_SLOTS = 8
DILATIONS = (1, 4, 16)
N_DIL = 3
ATT_BLOCK = 128
ATT_Q = ATT_SLOTS * N_DIL * HEAD_DIM
ATT_KV = ATT_SLOTS * HEAD_DIM
ALIBI_MAX_EXP = 8.0
CONV_CH = 512
CONV_K = 31
MLSTM_HEADS = 4
MLSTM_DH = 128
MLSTM_W = MLSTM_HEADS * MLSTM_DH
MLSTM_CONV_K = 4
MLSTM_CHUNK = 128
N_GROUPS_MOE = 4
EXPERTS_PER_GROUP = 8
N_EXPERTS = N_GROUPS_MOE * EXPERTS_PER_GROUP
D_EXPERT = 256
MOE_BLOCK = 128
PAIRS_PER_GROUP = EXPERTS_PER_GROUP * (EXPERTS_PER_GROUP - 1) // 2
N_CLASSES = N_GROUPS_MOE * PAIRS_PER_GROUP
PLE_DIM = 256
EPS = 1e-6
NEG = -1e30

LANES = 128
SUBLANES = 8
ROW_TILES = D_MODEL // LANES
PAIR_TILES = 2 * ROW_TILES
EXPERT_BLOCKS = 2
VMEM_LIMIT = 56 * 1024 * 1024

OFF_GATE = 0
OFF_QA = 3 * D_MODEL
OFF_KA = OFF_QA + ATT_Q
OFF_VA = OFF_KA + ATT_KV
OFF_CA = OFF_VA + ATT_KV
OFF_CG = OFF_CA + CONV_CH
OFF_QM = OFF_CG + CONV_CH
OFF_KM = OFF_QM + MLSTM_W
OFF_VM = OFF_KM + MLSTM_W
OFF_OM = OFF_VM + MLSTM_W
OFF_IF = OFF_OM + MLSTM_W
PROJ_N = OFF_IF + LANES
PROJ_TN = PROJ_N // 3
ORIG_SMALL = ATT_Q + 2 * ATT_KV + 2 * CONV_CH + 4 * MLSTM_W

ATT_TILE = ATT_BLOCK * DILATIONS[-1]
ATT_UNROLL = 16


def _cparams(sem, vmem=VMEM_LIMIT):
    return pltpu.CompilerParams(dimension_semantics=sem, vmem_limit_bytes=vmem)


def _rms(x, g):
    r = lax.rsqrt(jnp.mean(x * x, axis=-1, keepdims=True) + EPS)
    return (x * r) * g


def _sigmoid(x):
    return 1.0 / (1.0 + jnp.exp(-x))


def _log_sigmoid(x):
    return jnp.minimum(x, 0.0) - jnp.log(1.0 + jnp.exp(-jnp.abs(x)))


def _inproj_kernel(x_ref, g_ref, w_ref, o_ref):
    h = _rms(x_ref[...], g_ref[...]).astype(BF16)
    for j in range(PROJ_N // PROJ_TN):
        cols = slice(j * PROJ_TN, (j + 1) * PROJ_TN)
        o_ref[:, cols] = jnp.dot(h, w_ref[:, cols], preferred_element_type=F32)


def _inproj(x2, g, w, tm=256):
    t = x2.shape[0]
    return pl.pallas_call(
        _inproj_kernel,
        grid=(t // tm,),
        in_specs=[pl.BlockSpec((tm, D_MODEL), lambda i: (i, 0)),
                  pl.BlockSpec((1, D_MODEL), lambda i: (0, 0)),
                  pl.BlockSpec((D_MODEL, PROJ_N), lambda i: (0, 0), pipeline_mode=pl.Buffered(1))],
        out_specs=pl.BlockSpec((tm, PROJ_N), lambda i: (i, 0)),
        out_shape=jax.ShapeDtypeStruct((t, PROJ_N), F32),
        compiler_params=_cparams(("arbitrary",)),
    )(x2, g, w)


def _att_bias_table():
    j = np.arange(1, ATT_SLOTS * N_DIL + 1, dtype=np.float64)
    slopes = (2.0 ** (-ALIBI_MAX_EXP * j / (ATT_SLOTS * N_DIL))).reshape(N_DIL, ATT_SLOTS)
    qi = np.arange(ATT_BLOCK)[:, None] + ATT_BLOCK
    ki = np.arange(2 * ATT_BLOCK)[None, :]
    dist = qi - ki
    valid = (dist >= 0) & (dist <= ATT_BLOCK)
    tab = np.zeros((ATT_SLOTS // 2, N_DIL * 4, ATT_BLOCK, 2 * ATT_BLOCK), np.float32)
    for hp in range(ATT_SLOTS // 2):
        for g, dil in enumerate(DILATIONS):
            for h in range(2):
                slope = np.float32(slopes[g, 2 * hp + h])
                bias = -(slope * (dil * dist).astype(np.float32))
                tab[hp, g * 4 + h * 2 + 0] = np.where(valid, bias, NEG)
                tab[hp, g * 4 + h * 2 + 1] = np.where(valid & (ki >= ATT_BLOCK), bias, NEG)
    return tab


def _att_kernel(q0_ref, q1_ref, q2_ref, kc_ref, kp_ref, vc_ref, vp_ref, bias_ref, o_ref,
                qs, ks0, ks1, ks2, vs0, vs1, vs2, og, lg, onat, lnat, tmp):
    i = pl.program_id(2)
    q_refs = (q0_ref, q1_ref, q2_ref)
    k_scr = (ks0, ks1, ks2)
    v_scr = (vs0, vs1, vs2)
    scale = HEAD_DIM ** -0.5

    for g, dil in enumerate(DILATIONS[:2]):
        lt = ATT_TILE // dil
        cs = ATT_BLOCK + lt
        for r in range(dil):
            cur = pl.ds(r, lt, stride=dil) if dil > 1 else pl.ds(0, lt)
            prv = (pl.ds(ATT_TILE - ATT_BLOCK * dil + r, ATT_BLOCK, stride=dil) if dil > 1
                   else pl.ds(ATT_TILE - ATT_BLOCK, ATT_BLOCK))
            qs[g, r * lt:(r + 1) * lt, :] = (q_refs[g][cur, :] * scale).astype(BF16)
            k_scr[g][r * cs + ATT_BLOCK:(r + 1) * cs, :] = kc_ref[cur, :].astype(BF16)
            k_scr[g][r * cs:r * cs + ATT_BLOCK, :] = kp_ref[prv, :].astype(BF16)
            v_scr[g][r * cs + ATT_BLOCK:(r + 1) * cs, :] = vc_ref[cur, :].astype(BF16)
            v_scr[g][r * cs:r * cs + ATT_BLOCK, :] = vp_ref[prv, :].astype(BF16)

    sub, quarter = DILATIONS[1], ATT_TILE // DILATIONS[1]

    def classes16(src_ref, put, mul=None):
        for r4 in range(sub):
            tmp[r4 * quarter:(r4 + 1) * quarter, :] = src_ref[pl.ds(r4, quarter, stride=sub), :]
        for r in range(DILATIONS[2]):
            v = tmp[pl.ds((r % sub) * quarter + r // sub, ATT_BLOCK, stride=sub), :]
            put(r, (v if mul is None else v * mul).astype(BF16))

    cs16 = 2 * ATT_BLOCK

    def put_q(r, v):
        qs[2, r * ATT_BLOCK:(r + 1) * ATT_BLOCK, :] = v

    def put_at(scr, off):
        def put(r, v):
            scr[r * cs16 + off:r * cs16 + off + ATT_BLOCK, :] = v
        return put

    classes16(q2_ref, put_q, scale)
    classes16(kc_ref, put_at(ks2, ATT_BLOCK))
    classes16(kp_ref, put_at(ks2, 0))
    classes16(vc_ref, put_at(vs2, ATT_BLOCK))
    classes16(vp_ref, put_at(vs2, 0))

    lane = lax.broadcasted_iota(I32, (ATT_BLOCK, LANES), 1)
    head0 = lane < HEAD_DIM

    for g, dil in enumerate(DILATIONS):
        nj = ATT_TILE // dil // ATT_BLOCK

        def block(t, carry, g=g, nj=nj):
            r = t // nj
            j = t - r * nj
            qoff = pl.multiple_of(t * ATT_BLOCK, ATT_BLOCK)
            koff = pl.multiple_of((r * (nj + 1) + j) * ATT_BLOCK, ATT_BLOCK)
            first = jnp.logical_and(i == 0, j == 0).astype(I32)
            qb = qs[g, pl.ds(qoff, ATT_BLOCK), :]
            kb = k_scr[g][pl.ds(koff, 2 * ATT_BLOCK), :]
            vb = v_scr[g][pl.ds(koff, 2 * ATT_BLOCK), :]
            outs, lses = [], []
            for h in range(2):
                qh = jnp.where(head0 if h == 0 else jnp.logical_not(head0), qb, jnp.zeros_like(qb))
                s = lax.dot_general(qh, kb, (((1,), (1,)), ((), ())), preferred_element_type=F32)
                s = s + bias_ref[g * 4 + h * 2 + first]
                m = jnp.max(s, axis=-1, keepdims=True)
                p = jnp.exp(s - m)
                den = jnp.sum(p, axis=-1, keepdims=True)
                o = jnp.dot(p.astype(BF16), vb, preferred_element_type=F32) / den
                outs.append(o)
                lses.append(jnp.broadcast_to(m + jnp.log(den), (ATT_BLOCK, LANES)))
            og[g, pl.ds(qoff, ATT_BLOCK), :] = jnp.where(head0, outs[0], outs[1])
            lg[g, pl.ds(qoff, ATT_BLOCK), :] = jnp.where(head0, lses[0], lses[1])
            return carry

        lax.fori_loop(0, ATT_TILE // ATT_BLOCK, block, 0, unroll=ATT_UNROLL)

    for r in range(sub):
        onat[0, pl.ds(r, quarter, stride=sub), :] = og[1, r * quarter:(r + 1) * quarter, :]
        lnat[0, pl.ds(r, quarter, stride=sub), :] = lg[1, r * quarter:(r + 1) * quarter, :]
    for src, dst in ((og, onat), (lg, lnat)):
        for r in range(DILATIONS[2]):
            tmp[pl.ds((r % sub) * quarter + r // sub, ATT_BLOCK, stride=sub), :] = (
                src[2, r * ATT_BLOCK:(r + 1) * ATT_BLOCK, :])
        for r4 in range(sub):
            dst[1, pl.ds(r4, quarter, stride=sub), :] = tmp[r4 * quarter:(r4 + 1) * quarter, :]

    rc = 256

    def merge(c, carry):
        rows = pl.ds(pl.multiple_of(c * rc, rc), rc)
        l0, l1, l2 = lg[0, rows, :], lnat[0, rows, :], lnat[1, rows, :]
        mx = jnp.maximum(jnp.maximum(l0, l1), l2)
        w0, w1, w2 = jnp.exp(l0 - mx), jnp.exp(l1 - mx), jnp.exp(l2 - mx)
        tot = w0 + w1 + w2
        o = (w0 / tot) * og[0, rows, :] + (w1 / tot) * onat[0, rows, :] + (w2 / tot) * onat[1, rows, :]
        o_ref[rows, :] = o.astype(o_ref.dtype)
        return carry

    lax.fori_loop(0, ATT_TILE // rc, merge, 0)


def _attention(proj, bias_tab, batch, seq):
    t = proj.shape[0]
    nt = seq // ATT_TILE
    cb = lambda off: off // LANES

    def cur(col):
        return pl.BlockSpec((ATT_TILE, LANES), lambda hp, b, i, col=col: (b * nt + i, col + hp))

    def prev(col):
        return pl.BlockSpec((ATT_TILE, LANES),
                            lambda hp, b, i, col=col: (b * nt + jnp.maximum(i - 1, 0), col + hp))

    npairs = ATT_SLOTS // 2
    kv_rows = [ATT_TILE + ATT_BLOCK * d for d in DILATIONS]
    return pl.pallas_call(
        _att_kernel,
        grid=(npairs, batch, nt),
        in_specs=[cur(cb(OFF_QA)), cur(cb(OFF_QA) + npairs), cur(cb(OFF_QA) + 2 * npairs),
                  cur(cb(OFF_KA)), prev(cb(OFF_KA)), cur(cb(OFF_VA)), prev(cb(OFF_VA)),
                  pl.BlockSpec((None, N_DIL * 4, ATT_BLOCK, 2 * ATT_BLOCK), lambda hp, b, i: (hp, 0, 0, 0))],
        out_specs=pl.BlockSpec((ATT_TILE, LANES), lambda hp, b, i: (b * nt + i, hp)),
        out_shape=jax.ShapeDtypeStruct((t, ATT_KV), BF16),
        scratch_shapes=([pltpu.VMEM((N_DIL, ATT_TILE, LANES), BF16)]
                        + [pltpu.VMEM((n, LANES), BF16) for n in kv_rows]
                        + [pltpu.VMEM((n, LANES), BF16) for n in kv_rows]
                        + [pltpu.VMEM((N_DIL, ATT_TILE, LANES), F32)] * 2
                        + [pltpu.VMEM((N_DIL - 1, ATT_TILE, LANES), F32)] * 2
                        + [pltpu.VMEM((ATT_TILE, LANES), F32)]),
        compiler_params=_cparams(("arbitrary", "arbitrary", "arbitrary")),
    )(proj, proj, proj, proj, proj, proj, proj, bias_tab)


CONV_TILE = 256
CONV_HALO = 32
CONV_ROWS = 32


def _conv_kernel(a_ref, gt_ref, ha_ref, hg_ref, w_ref, b_ref, lng_ref, lnb_ref, o_ref, ybuf, win):
    i = pl.program_id(1)
    halo = ha_ref[...] * _sigmoid(hg_ref[...])
    ybuf[0:CONV_HALO, :] = jnp.where(i > 0, halo, jnp.zeros_like(halo))
    ybuf[CONV_HALO:, :] = a_ref[...] * _sigmoid(gt_ref[...])
    first = CONV_HALO - (CONV_K - 1)
    for c in range(CONV_TILE // CONV_ROWS):
        for j in range(SUBLANES):
            span = ((CONV_K - 1 - j) // SUBLANES) * SUBLANES + CONV_ROWS
            lo = c * CONV_ROWS + first + j
            win[j, 0:span, :] = ybuf[lo:lo + span, :]
        acc = jnp.broadcast_to(b_ref[...], (CONV_ROWS, CONV_CH))
        for k in range(CONV_K):
            j, a = k % SUBLANES, k // SUBLANES
            acc = acc + w_ref[k:k + 1, :] * win[j, a * SUBLANES:a * SUBLANES + CONV_ROWS, :]
        mu = jnp.mean(acc, axis=-1, keepdims=True)
        cen = acc - mu
        var = jnp.mean(cen * cen, axis=-1, keepdims=True)
        y = (cen * lax.rsqrt(var + EPS)) * lng_ref[...] + lnb_ref[...]
        o_ref[c * CONV_ROWS:(c + 1) * CONV_ROWS, :] = (y * _sigmoid(y)).astype(o_ref.dtype)


def _conv_module(proj, w, b, ln_g, ln_b, batch, seq):
    t = proj.shape[0]
    nt = seq // CONV_TILE
    hb = CONV_TILE // CONV_HALO
    ca, cg = OFF_CA // CONV_CH, OFF_CG // CONV_CH
    cur = lambda col: pl.BlockSpec((CONV_TILE, CONV_CH), lambda bb, i, col=col: (bb * nt + i, col))
    halo = lambda col: pl.BlockSpec(
        (CONV_HALO, CONV_CH), lambda bb, i, col=col: (jnp.maximum((bb * nt + i) * hb - 1, 0), col))
    vec = pl.BlockSpec((1, CONV_CH), lambda bb, i: (0, 0))
    return pl.pallas_call(
        _conv_kernel,
        grid=(batch, nt),
        in_specs=[cur(ca), cur(cg), halo(ca), halo(cg),
                  pl.BlockSpec((CONV_HALO, CONV_CH), lambda bb, i: (0, 0)), vec, vec, vec],
        out_specs=pl.BlockSpec((CONV_TILE, CONV_CH), lambda bb, i: (bb * nt + i, 0)),
        out_shape=jax.ShapeDtypeStruct((t, CONV_CH), BF16),
        scratch_shapes=[pltpu.VMEM((CONV_HALO + CONV_TILE, CONV_CH), F32),
                        pltpu.VMEM((SUBLANES, CONV_HALO + CONV_ROWS, CONV_CH), F32)],
        compiler_params=_cparams(("arbitrary", "arbitrary")),
    )(proj, proj, proj, proj, w, b, ln_g, ln_b)


def _mlstm_kernel(q_ref, k_ref, v_ref, om_ref, gate_ref, cw_ref, cb_ref, gb_ref, o_ref,
                  ext, ct_s, n_s, m_s, *, batch):
    c = pl.program_id(0)
    lc, dh, nh = MLSTM_CHUNK, MLSTM_DH, MLSTM_HEADS
    assert lc == dh == LANES
    pad = SUBLANES

    @pl.when(c == 0)
    def _():
        ext[:, 0:pad, :] = jnp.zeros((batch, pad, 2 * MLSTM_W), F32)
        ct_s[...] = jnp.zeros_like(ct_s)
        n_s[...] = jnp.zeros_like(n_s)
        m_s[...] = jnp.zeros_like(m_s)

    @pl.when(c > 0)
    def _():
        ext[:, 0:pad, :] = ext[:, lc:lc + pad, :]

    ext[:, pad:, 0:MLSTM_W] = q_ref[...]
    ext[:, pad:, MLSTM_W:] = k_ref[...]

    row = lax.broadcasted_iota(I32, (lc, lc), 0)
    col = lax.broadcasted_iota(I32, (lc, lc), 1)
    causal = row >= col
    tril = causal.astype(F32)
    is_f = jnp.logical_and(col >= nh, col < 2 * nh)

    for b in range(batch):
        acc = jnp.broadcast_to(cb_ref[...], (lc, 2 * MLSTM_W))
        for k in range(MLSTM_CONV_K):
            lo = pad - (MLSTM_CONV_K - 1) + k
            acc = acc + cw_ref[k:k + 1, :] * ext[b, lo:lo + lc, :]
        qk = acc * _sigmoid(acc)

        gpre = gate_ref[b] + gb_ref[...]
        lf = jnp.where(is_f, _log_sigmoid(gpre), jnp.zeros_like(gpre))
        bcum = jnp.dot(tril, lf, precision=HIGHEST, preferred_element_type=F32)
        gpre_t = gpre.T
        bcum_t = bcum.T

        for h in range(nh):
            qh = qk[:, h * dh:(h + 1) * dh]
            kh = qk[:, MLSTM_W + h * dh:MLSTM_W + (h + 1) * dh] * (dh ** -0.5)
            vh = v_ref[b, :, h * dh:(h + 1) * dh]
            qb, kb, vb = qh.astype(BF16), kh.astype(BF16), vh.astype(BF16)
            bq = jnp.broadcast_to(bcum[:, nh + h:nh + h + 1], (lc, lc))
            br = bcum_t[nh + h:nh + h + 1, :]
            igr = gpre_t[h:h + 1, :]
            igc = jnp.broadcast_to(gpre[:, h:h + 1], (lc, lc))
            m_prev = m_s[b, h]
            ct = ct_s[b, h]
            n_row = n_s[b, h]

            dmat = jnp.where(causal, (bq - br) + igr, NEG)
            a_inter = bq + m_prev
            m_t = jnp.maximum(a_inter, jnp.max(dmat, axis=-1, keepdims=True))
            w_intra = jnp.exp(dmat - m_t)
            sc = lax.dot_general(qb, kb, (((1,), (1,)), ((), ())), preferred_element_type=F32) * w_intra
            s_inter = jnp.exp(a_inter - m_t)
            num = (jnp.dot(sc.astype(BF16), vb, preferred_element_type=F32)
                   + s_inter * jnp.dot(qb, ct.astype(BF16), preferred_element_type=F32))
            den = (jnp.sum(sc, axis=-1, keepdims=True)
                   + s_inter * jnp.sum(qh * n_row, axis=-1, keepdims=True))
            hout = num / jnp.maximum(jnp.abs(den), jnp.exp(-m_t))
            og = _sigmoid(om_ref[b, :, h * dh:(h + 1) * dh])
            o_ref[b, :, h * dh:(h + 1) * dh] = (og * hout).astype(o_ref.dtype)

            gtot = bq[lc - 1:lc, :]
            wst = (gtot - bq) + igc
            m_loc = jnp.max(wst, axis=0, keepdims=True)
            ek = jnp.exp(wst - m_loc) * kh
            c_loc = jnp.dot(ek.T.astype(BF16), vb, preferred_element_type=F32)
            n_loc = jnp.sum(ek, axis=0, keepdims=True)
            m_new = jnp.maximum(gtot + m_prev, m_loc)
            fa = jnp.exp(gtot + m_prev - m_new)
            fb = jnp.exp(m_loc - m_new)
            ct_s[b, h] = fa * ct + fb * c_loc
            n_s[b, h] = fa * n_row + fb * n_loc
            m_s[b, h] = m_new


def _mlstm(proj3, cw, cb, gb, batch, seq):
    nc = seq // MLSTM_CHUNK
    wide = lambda off: pl.BlockSpec((batch, MLSTM_CHUNK, MLSTM_W), lambda c, off=off: (0, c, off // MLSTM_W))
    full = lambda shape: pl.BlockSpec(shape, lambda c: (0,) * len(shape))
    return pl.pallas_call(
        functools.partial(_mlstm_kernel, batch=batch),
        grid=(nc,),
        in_specs=[wide(OFF_QM), wide(OFF_KM), wide(OFF_VM), wide(OFF_OM),
                  pl.BlockSpec((batch, MLSTM_CHUNK, LANES), lambda c: (0, c, OFF_IF // LANES)),
                  full((SUBLANES, 2 * MLSTM_W)), full((1, 2 * MLSTM_W)), full((1, LANES))],
        out_specs=pl.BlockSpec((batch, MLSTM_CHUNK, MLSTM_W), lambda c: (0, c, 0)),
        out_shape=jax.ShapeDtypeStruct((batch, seq, MLSTM_W), BF16),
        scratch_shapes=[pltpu.VMEM((batch, MLSTM_CHUNK + SUBLANES, 2 * MLSTM_W), F32),
                        pltpu.VMEM((batch, MLSTM_HEADS, MLSTM_DH, MLSTM_DH), F32),
                        pltpu.VMEM((batch, MLSTM_HEADS, 1, MLSTM_DH), F32),
                        pltpu.VMEM((batch, MLSTM_HEADS, 1, LANES), F32)],
        compiler_params=_cparams(("arbitrary",)),
    )(proj3, proj3, proj3, proj3, proj3, cw, cb, gb)


MERGE_ROWS = 256


def _merge_kernel(x_ref, gate_ref, ya_ref, yc_ref, ym_ref, wa_ref, wc_ref, wm_ref, wo_ref,
                  gf_ref, wr_ref, br_ref, xo_ref, h2_ref, lg_ref, *, tm):
    d = D_MODEL
    for r0 in range(0, tm, MERGE_ROWS):
        rows = slice(r0, r0 + MERGE_ROWS)
        branch = lambda y_ref, w_ref: jnp.dot(y_ref[rows, :], w_ref[...], preferred_element_type=F32)
        merged = (_sigmoid(gate_ref[rows, 0:d]) * branch(ya_ref, wa_ref)
                  + _sigmoid(gate_ref[rows, d:2 * d]) * branch(yc_ref, wc_ref)
                  + _sigmoid(gate_ref[rows, 2 * d:3 * d]) * branch(ym_ref, wm_ref))
        xn = x_ref[rows, :] + jnp.dot(merged.astype(BF16), wo_ref[...], preferred_element_type=F32)
        xo_ref[rows, :] = xn
        h2 = _rms(xn, gf_ref[...])
        lg_ref[rows, :] = jnp.dot(h2.astype(BF16), wr_ref[...], preferred_element_type=F32) + br_ref[...]
        for s in range(ROW_TILES):
            h2_ref[pl.ds(r0 * ROW_TILES + s, MERGE_ROWS, stride=ROW_TILES), :] = h2[:, s * LANES:(s + 1) * LANES]


def _merge(x2, proj, y_att, y_conv, y_m, wa, wc, wm, wo, g_ffn, w_r, b_r, tm=2 * MERGE_ROWS):
    t = x2.shape[0]
    rows = lambda n: pl.BlockSpec((tm, n), lambda i: (i, 0))
    full = lambda a: pl.BlockSpec(a.shape, lambda i: (0, 0))
    return pl.pallas_call(
        functools.partial(_merge_kernel, tm=tm),
        grid=(t // tm,),
        in_specs=[rows(D_MODEL), rows(3 * D_MODEL), rows(ATT_KV), rows(CONV_CH), rows(MLSTM_W),
                  full(wa), full(wc), full(wm), full(wo), full(g_ffn), full(w_r), full(b_r)],
        out_specs=[rows(D_MODEL), pl.BlockSpec((tm * ROW_TILES, LANES), lambda i: (i, 0)), rows(LANES)],
        out_shape=[jax.ShapeDtypeStruct((t, D_MODEL), F32),
                   jax.ShapeDtypeStruct((t * ROW_TILES, LANES), F32),
                   jax.ShapeDtypeStruct((t, LANES), F32)],
        compiler_params=_cparams(("arbitrary",)),
    )(x2, proj, y_att, y_conv, y_m, wa, wc, wm, wo, g_ffn, w_r, b_r)


ROUTER_TILE = 512


def _router_kernel(lg_ref, idx_ref, wt_ref, cnt_ref, run):
    n = ROUTER_TILE
    ng, ne = N_GROUPS_MOE, EXPERTS_PER_GROUP

    @pl.when(pl.program_id(0) == 0)
    def _():
        run[...] = jnp.zeros_like(run)

    lt = lg_ref[...].T
    lgrp = lt[0:ng, :]
    gmax = jnp.max(lgrp, axis=0, keepdims=True)
    grow = lax.broadcasted_iota(I32, (ng, n), 0)
    g_sel = jnp.min(jnp.where(lgrp == gmax, grow, ng), axis=0, keepdims=True)
    p_g = 1.0 / jnp.sum(jnp.exp(lgrp - gmax), axis=0, keepdims=True)

    le_sel = jnp.zeros((ne, n), F32)
    for g in range(ng):
        le_sel = jnp.where(g_sel == g, lt[(g + 1) * ne:(g + 2) * ne, :], le_sel)
    erow = lax.broadcasted_iota(I32, (ne, n), 0)
    v1 = jnp.max(le_sel, axis=0, keepdims=True)
    i1 = jnp.min(jnp.where(le_sel == v1, erow, ne), axis=0, keepdims=True)
    rest = jnp.where(erow == i1, -jnp.inf, le_sel)
    v2 = jnp.max(rest, axis=0, keepdims=True)
    i2 = jnp.min(jnp.where(rest == v2, erow, ne), axis=0, keepdims=True)
    e2 = jnp.exp(v2 - v1)
    w1 = p_g * (1.0 / (1.0 + e2))
    w2 = p_g * (e2 / (1.0 + e2))
    lo = jnp.minimum(i1, i2)
    hi = jnp.maximum(i1, i2)
    pair = ((lo * (2 * ne - 1 - lo)) >> 1) + (hi - lo - 1)
    cls = g_sel * PAIRS_PER_GROUP + pair
    first_is_lo = i1 < i2
    w_lo = jnp.where(first_is_lo, w1, w2)
    w_hi = jnp.where(first_is_lo, w2, w1)

    hit = lax.broadcasted_iota(I32, (LANES, n), 0) == cls
    onehot = jnp.where(hit, 1.0, 0.0)
    r_i = lax.broadcasted_iota(I32, (n, n), 0)
    c_i = lax.broadcasted_iota(I32, (n, n), 1)
    before = jnp.where(r_i < c_i, 1.0, 0.0).astype(BF16)
    cnt = jnp.dot(onehot.astype(BF16), before, preferred_element_type=F32) + run[:, 0:1]
    rank = jnp.sum(jnp.where(hit, cnt, 0.0), axis=0, keepdims=True)
    run[...] = run[...] + jnp.sum(onehot, axis=1, keepdims=True)

    idx_ref[...] = jnp.concatenate([cls, rank.astype(I32), jnp.zeros((SUBLANES - 2, n), I32)], axis=0)
    wt_ref[...] = jnp.concatenate([w_lo, w_hi, jnp.zeros((SUBLANES - 2, n), F32)], axis=0)
    cnt_ref[...] = run[...]


def _router(logits):
    t = logits.shape[0]
    n = ROUTER_TILE
    return pl.pallas_call(
        _router_kernel,
        grid=(t // n,),
        in_specs=[pl.BlockSpec((n, LANES), lambda i: (i, 0))],
        out_specs=[pl.BlockSpec((SUBLANES, n), lambda i: (0, i)),
                   pl.BlockSpec((SUBLANES, n), lambda i: (0, i)),
                   pl.BlockSpec((LANES, LANES), lambda i: (0, 0))],
        out_shape=[jax.ShapeDtypeStruct((SUBLANES, t), I32),
                   jax.ShapeDtypeStruct((SUBLANES, t), F32),
                   jax.ShapeDtypeStruct((LANES, LANES), F32)],
        scratch_shapes=[pltpu.VMEM((LANES, LANES), F32)],
        compiler_params=_cparams(("arbitrary",)),
    )(logits)


DISPATCH_TILE = 512


def _dispatch_kernel(dest_ref, x_ref, xs_in_ref, xs_ref, sem):
    del xs_in_ref
    n = DISPATCH_TILE

    def row_copy(t):
        d = pl.multiple_of(dest_ref[0, 0, t], ROW_TILES)
        src = x_ref.at[pl.ds(pl.multiple_of(t * ROW_TILES, ROW_TILES), ROW_TILES), :]
        return pltpu.make_async_copy(src, xs_ref.at[pl.ds(d, ROW_TILES), :], sem)

    def issue(t, carry):
        row_copy(t).start()
        return carry

    lax.fori_loop(0, n, issue, 0, unroll=8)

    def drain(t, carry):
        row_copy(t).wait()
        return carry

    lax.fori_loop(0, n, drain, 0, unroll=8)


def _dispatch(h2t, dest_tiles, p_len):
    t = h2t.shape[0] // ROW_TILES
    n = DISPATCH_TILE
    xs0 = jnp.zeros((p_len * ROW_TILES, LANES), F32)
    return pl.pallas_call(
        _dispatch_kernel,
        grid=(t // n,),
        in_specs=[pl.BlockSpec((1, 1, n), lambda i: (i, 0, 0), memory_space=pltpu.SMEM),
                  pl.BlockSpec((n * ROW_TILES, LANES), lambda i: (i, 0)),
                  pl.BlockSpec(memory_space=pl.ANY)],
        out_specs=pl.BlockSpec(memory_space=pl.ANY),
        out_shape=jax.ShapeDtypeStruct((p_len * ROW_TILES, LANES), F32),
        scratch_shapes=[pltpu.SemaphoreType.DMA(())],
        input_output_aliases={2: 0},
        compiler_params=_cparams(("arbitrary",)),
    )(dest_tiles, h2t, xs0)


def _expert_kernel(lo_ref, hi_ref, nu_ref, xs_ref, *refs):
    del lo_ref, hi_ref
    w_refs, y_ref = refs[:-1], refs[-1]
    first = pl.program_id(0) * EXPERT_BLOCKS
    used = nu_ref[0] - first

    def run(sub):
        rows_in = sub * MOE_BLOCK * ROW_TILES
        rows_out = sub * MOE_BLOCK * PAIR_TILES
        xb = jnp.concatenate(
            [xs_ref[pl.ds(rows_in + s, MOE_BLOCK, stride=ROW_TILES), :] for s in range(ROW_TILES)],
            axis=-1).astype(BF16)
        for half in range(2):
            wg, wu, wd = w_refs[sub * 6 + half * 3:sub * 6 + half * 3 + 3]
            hg = jnp.dot(xb, wg[...], preferred_element_type=F32)
            hu = jnp.dot(xb, wu[...], preferred_element_type=F32)
            hb = (hg * _sigmoid(hg)) * hu
            y = jnp.dot(hb.astype(BF16), wd[...], preferred_element_type=F32)
            for s in range(ROW_TILES):
                y_ref[pl.ds(rows_out + half * ROW_TILES + s, MOE_BLOCK, stride=PAIR_TILES), :] = (
                    y[:, s * LANES:(s + 1) * LANES])

    def clear(sub):
        n = MOE_BLOCK * PAIR_TILES
        y_ref[sub * n:(sub + 1) * n, :] = jnp.zeros((n, LANES), F32)

    for count in range(EXPERT_BLOCKS + 1):
        @pl.when(jnp.clip(used, 0, EXPERT_BLOCKS) == count)
        def _(count=count):
            for sub in range(EXPERT_BLOCKS):
                (run if sub < count else clear)(sub)


def _experts(blk_lo, blk_hi, n_used, xs, wg, wu, wd):
    nb = blk_lo.shape[0]
    last = lambda b, nu: jnp.minimum(b, nu[0] - 1)
    xrows = pl.BlockSpec((EXPERT_BLOCKS * MOE_BLOCK * ROW_TILES, LANES), lambda i, lo, hi, nu: (i, 0))
    yrows = pl.BlockSpec((EXPERT_BLOCKS * MOE_BLOCK * PAIR_TILES, LANES), lambda i, lo, hi, nu: (i, 0))
    up, down = (D_MODEL, D_EXPERT), (D_EXPERT, D_MODEL)

    def weight(shape, sub, high):
        def index(i, lo, hi, nu):
            b = last(i * EXPERT_BLOCKS + sub, nu)
            return ((hi if high else lo)[b], 0, 0)
        return pl.BlockSpec((None,) + shape, index)

    w_specs, w_args = [], []
    for sub in range(EXPERT_BLOCKS):
        for high in (False, True):
            w_specs += [weight(up, sub, high), weight(up, sub, high), weight(down, sub, high)]
            w_args += [wg, wu, wd]
    return pl.pallas_call(
        _expert_kernel,
        grid_spec=pltpu.PrefetchScalarGridSpec(
            num_scalar_prefetch=3,
            grid=(nb // EXPERT_BLOCKS,),
            in_specs=[xrows] + w_specs,
            out_specs=yrows),
        out_shape=jax.ShapeDtypeStruct((xs.shape[0] * 2, LANES), F32),
        compiler_params=_cparams(("arbitrary",)),
    )(blk_lo, blk_hi, n_used, xs, *w_args)


COMBINE_TILE = 512
COMBINE_ROWS = 256


def _combine_kernel(dcur_ref, dnxt_ref, x_ref, wt_ref, p_ref, y_hbm, gp_ref, wpg_ref, wpp_ref, gfin_ref,
                    o_ref, ybuf, sems, *, final):
    n = COMBINE_TILE
    i = pl.program_id(0)
    slot = i % 2

    def row_copy(dref, sl, t):
        d = pl.multiple_of(dref[0, 0, t], PAIR_TILES)
        dst = ybuf.at[sl, pl.ds(pl.multiple_of(t * PAIR_TILES, PAIR_TILES), PAIR_TILES), :]
        return pltpu.make_async_copy(y_hbm.at[pl.ds(d, PAIR_TILES), :], dst, sems.at[sl])

    def issue(dref, sl):
        def body(t, carry):
            row_copy(dref, sl, t).start()
            return carry
        lax.fori_loop(0, n, body, 0, unroll=8)

    @pl.when(i == 0)
    def _():
        issue(dcur_ref, slot)

    @pl.when(i + 1 < pl.num_programs(0))
    def _():
        issue(dnxt_ref, 1 - slot)

    pes = [jnp.dot(p_ref[r0:r0 + COMBINE_ROWS, :].astype(BF16), wpp_ref[...], preferred_element_type=F32)
           for r0 in range(0, n, COMBINE_ROWS)]

    def drain(t, carry):
        row_copy(dcur_ref, slot, t).wait()
        return carry

    lax.fori_loop(0, n, drain, 0, unroll=8)

    def rows_of(half, r0):
        return jnp.concatenate(
            [ybuf[slot, pl.ds(r0 * PAIR_TILES + half * ROW_TILES + s, COMBINE_ROWS, stride=PAIR_TILES), :]
             for s in range(ROW_TILES)], axis=-1)

    for part, r0 in enumerate(range(0, n, COMBINE_ROWS)):
        rows = slice(r0, r0 + COMBINE_ROWS)
        xn = x_ref[rows, :] + (wt_ref[rows, 0:1] * rows_of(0, r0) + wt_ref[rows, 1:2] * rows_of(1, r0))
        hp = _rms(xn, gp_ref[...])
        gate = _sigmoid(jnp.dot(hp.astype(BF16), wpg_ref[...], preferred_element_type=F32))
        xo = xn + gate * pes[part]
        if final:
            xo = _rms(xo, gfin_ref[...])
        o_ref[rows, :] = xo


def _combine(dest_tiles, x2, wts, p2, y, g_ple, wpg, wpp, g_final, final):
    t = x2.shape[0]
    n = COMBINE_TILE
    nt = t // n
    rows = lambda w: pl.BlockSpec((n, w), lambda i: (i, 0))
    full = lambda a: pl.BlockSpec(a.shape, lambda i: (0, 0))
    return pl.pallas_call(
        functools.partial(_combine_kernel, final=final),
        grid=(nt,),
        in_specs=[pl.BlockSpec((1, 1, n), lambda i: (i, 0, 0), memory_space=pltpu.SMEM),
                  pl.BlockSpec((1, 1, n), lambda i: (jnp.minimum(i + 1, nt - 1), 0, 0), memory_space=pltpu.SMEM),
                  rows(D_MODEL), rows(2), rows(PLE_DIM),
                  pl.BlockSpec(memory_space=pl.ANY),
                  full(g_ple), full(wpg), full(wpp), full(g_final)],
        out_specs=rows(D_MODEL),
        out_shape=jax.ShapeDtypeStruct((t, D_MODEL), F32),
        scratch_shapes=[pltpu.VMEM((2, n * PAIR_TILES, LANES), F32), pltpu.SemaphoreType.DMA((2,))],
        compiler_params=_cparams(("arbitrary",)),
    )(dest_tiles, dest_tiles, x2, wts, p2, y, g_ple, wpg, wpp, g_final)


def _class_experts():
    lo, hi = [], []
    for g in range(N_GROUPS_MOE):
        for a in range(EXPERTS_PER_GROUP):
            for b in range(a + 1, EXPERTS_PER_GROUP):
                lo.append(g * EXPERTS_PER_GROUP + a)
                hi.append(g * EXPERTS_PER_GROUP + b)
    return np.asarray(lo, np.int32), np.asarray(hi, np.int32)


def _moe_plan(idx, cnt, t):
    counts = cnt[:N_CLASSES, 0].astype(I32)
    padded = (counts + MOE_BLOCK - 1) // MOE_BLOCK * MOE_BLOCK
    ends_p = jnp.cumsum(padded)
    pstart = ends_p - padded
    classes = jnp.arange(N_CLASSES, dtype=I32)
    dest = jnp.sum(jnp.where(idx[0][:, None] == classes, pstart, 0), axis=-1) + idx[1]
    p_len = t + N_CLASSES * MOE_BLOCK
    nb = p_len // MOE_BLOCK
    blk_start = jnp.arange(nb, dtype=I32) * MOE_BLOCK
    blk_cls = jnp.minimum(jnp.sum((ends_p[None, :] <= blk_start[:, None]).astype(I32), axis=1), N_CLASSES - 1)
    lo_tab, hi_tab = _class_experts()
    is_c = blk_cls[:, None] == classes
    blk_lo = jnp.sum(jnp.where(is_c, lo_tab, 0), axis=1).astype(I32)
    blk_hi = jnp.sum(jnp.where(is_c, hi_tab, 0), axis=1).astype(I32)
    n_used = (ends_p[-1:] // MOE_BLOCK).astype(I32)
    return dest, blk_lo, blk_hi, n_used, p_len


def kernel(x, p, g_mix, w_in, conv_w, conv_b, conv_ln_g, conv_ln_b, mlstm_conv_w, mlstm_conv_b,
           mlstm_i_bias, mlstm_f_bias, w_br_att, w_br_conv, w_br_mlstm, w_out, g_ffn,
           w_router_g, b_router_g, w_router_e, b_router_e, w_exp_gate, w_exp_up, w_exp_down,
           g_ple, w_ple_gate, w_ple_proj, g_final):
    batch, seq, d = x.shape
    depth = w_in.shape[0]
    t = batch * seq
    bias_tab = jnp.asarray(_att_bias_table())
    x2 = x.reshape(t, d)
    nr = N_GROUPS_MOE + N_EXPERTS
    for i in range(depth):
        w = w_in[i]
        wp = jnp.concatenate([w[:, ORIG_SMALL + 2 * MLSTM_HEADS:], w[:, :ORIG_SMALL],
                              w[:, ORIG_SMALL:ORIG_SMALL + 2 * MLSTM_HEADS],
                              jnp.zeros((d, LANES - 2 * MLSTM_HEADS), F32)], axis=1).astype(BF16)
        proj = _inproj(x2, g_mix[i][None, :], wp)

        y_att = _attention(proj, bias_tab, batch, seq)
        cw = jnp.concatenate([conv_w[i], jnp.zeros((CONV_HALO - CONV_K, CONV_CH), F32)], axis=0)
        y_conv = _conv_module(proj, cw, conv_b[i][None, :], conv_ln_g[i][None, :], conv_ln_b[i][None, :],
                              batch, seq)
        mw = jnp.concatenate([mlstm_conv_w[i], jnp.zeros((SUBLANES - MLSTM_CONV_K, 2 * MLSTM_W), F32)], axis=0)
        gb = jnp.concatenate([mlstm_i_bias[i], mlstm_f_bias[i],
                              jnp.zeros((LANES - 2 * MLSTM_HEADS,), F32)])[None, :]
        y_m = _mlstm(proj.reshape(batch, seq, PROJ_N), mw, mlstm_conv_b[i][None, :], gb, batch, seq)

        gpad = EXPERTS_PER_GROUP - N_GROUPS_MOE
        w_r = jnp.concatenate([w_router_g[i], jnp.zeros((d, gpad), F32), w_router_e[i],
                               jnp.zeros((d, LANES - nr - gpad), F32)], axis=1)
        b_r = jnp.concatenate([b_router_g[i], jnp.zeros((gpad,), F32), b_router_e[i],
                               jnp.zeros((LANES - nr - gpad,), F32)])[None, :]
        x2, h2t, logits = _merge(x2, proj, y_att, y_conv, y_m.reshape(t, MLSTM_W),
                                 w_br_att[i].astype(BF16), w_br_conv[i].astype(BF16),
                                 w_br_mlstm[i].astype(BF16), w_out[i].astype(BF16),
                                 g_ffn[i][None, :], w_r.astype(BF16), b_r)

        idx, wt, cnt = _router(logits)
        dest, blk_lo, blk_hi, n_used, p_len = _moe_plan(idx, cnt, t)
        xs = _dispatch(h2t, (dest * ROW_TILES).reshape(t // DISPATCH_TILE, 1, DISPATCH_TILE), p_len)
        ys = _experts(blk_lo, blk_hi, n_used, xs, w_exp_gate[i].astype(BF16), w_exp_up[i].astype(BF16),
                      w_exp_down[i].astype(BF16))
        x2 = _combine((dest * PAIR_TILES).reshape(t // COMBINE_TILE, 1, COMBINE_TILE), x2, wt[0:2].T,
                      p[i].reshape(t, PLE_DIM), ys, g_ple[i][None, :], w_ple_gate[i].astype(BF16),
                      w_ple_proj[i].astype(BF16), g_final[None, :], final=(i == depth - 1))
    return x2.reshape(batch, seq, d)
```

```python
import functools

import numpy as np
import jax
import jax.numpy as jnp
from jax import lax
from jax.experimental import pallas as pl
from jax.experimental.pallas import tpu as pltpu

F32 = jnp.float32
BF16 = jnp.bfloat16
I32 = jnp.int32
HIGHEST = lax.Precision.HIGHEST

D_MODEL = 1024
HEAD_DIM = 64
ATT_SLOTS = 8
DILATIONS = (1, 4, 16)
N_DIL = 3
ATT_BLOCK = 128
ATT_Q = ATT_SLOTS * N_DIL * HEAD_DIM
ATT_KV = ATT_SLOTS * HEAD_DIM
ALIBI_MAX_EXP = 8.0
CONV_CH = 512
CONV_K = 31
MLSTM_HEADS = 4
MLSTM_DH = 128
MLSTM_W = MLSTM_HEADS * MLSTM_DH
MLSTM_CONV_K = 4
MLSTM_CHUNK = 128
N_GROUPS_MOE = 4
EXPERTS_PER_GROUP = 8
N_EXPERTS = N_GROUPS_MOE * EXPERTS_PER_GROUP
D_EXPERT = 256
MOE_BLOCK = 128
PAIRS_PER_GROUP = EXPERTS_PER_GROUP * (EXPERTS_PER_GROUP - 1) // 2
N_CLASSES = N_GROUPS_MOE * PAIRS_PER_GROUP
PLE_DIM = 256
EPS = 1e-6
NEG = -1e30

LANES = 128
SUBLANES = 8
ROW_TILES = D_MODEL // LANES
PAIR_TILES = 2 * ROW_TILES
EXPERT_BLOCKS = 2
VMEM_LIMIT = 56 * 1024 * 1024

OFF_GATE = 0
OFF_QA = 3 * D_MODEL
OFF_KA = OFF_QA + ATT_Q
OFF_VA = OFF_KA + ATT_KV
OFF_CA = OFF_VA + ATT_KV
OFF_CG = OFF_CA + CONV_CH
OFF_QM = OFF_CG + CONV_CH
OFF_KM = OFF_QM + MLSTM_W
OFF_VM = OFF_KM + MLSTM_W
OFF_OM = OFF_VM + MLSTM_W
OFF_IF = OFF_OM + MLSTM_W
PROJ_N = OFF_IF + LANES
PROJ_TN = PROJ_N // 3
ORIG_SMALL = ATT_Q + 2 * ATT_KV + 2 * CONV_CH + 4 * MLSTM_W

ATT_TILE = ATT_BLOCK * DILATIONS[-1]
ATT_UNROLL = 16


def _cparams(sem, vmem=VMEM_LIMIT):
    return pltpu.CompilerParams(dimension_semantics=sem, vmem_limit_bytes=vmem)


def _rms(x, g):
    r = lax.rsqrt(jnp.mean(x * x, axis=-1, keepdims=True) + EPS)
    return (x * r) * g


def _sigmoid(x):
    return 1.0 / (1.0 + jnp.exp(-x))


def _log_sigmoid(x):
    return jnp.minimum(x, 0.0) - jnp.log(1.0 + jnp.exp(-jnp.abs(x)))


def _inproj_kernel(x_ref, g_ref, w_ref, o_ref):
    h = _rms(x_ref[...], g_ref[...]).astype(BF16)
    for j in range(PROJ_N // PROJ_TN):
        cols = slice(j * PROJ_TN, (j + 1) * PROJ_TN)
        o_ref[:, cols] = jnp.dot(h, w_ref[:, cols], preferred_element_type=F32)


def _inproj(x2, g, w, tm=256):
    t = x2.shape[0]
    return pl.pallas_call(
        _inproj_kernel,
        grid=(t // tm,),
        in_specs=[pl.BlockSpec((tm, D_MODEL), lambda i: (i, 0)),
                  pl.BlockSpec((1, D_MODEL), lambda i: (0, 0)),
                  pl.BlockSpec((D_MODEL, PROJ_N), lambda i: (0, 0), pipeline_mode=pl.Buffered(1))],
        out_specs=pl.BlockSpec((tm, PROJ_N), lambda i: (i, 0)),
        out_shape=jax.ShapeDtypeStruct((t, PROJ_N), F32),
        compiler_params=_cparams(("arbitrary",)),
    )(x2, g, w)


def _att_bias_table():
    j = np.arange(1, ATT_SLOTS * N_DIL + 1, dtype=np.float64)
    slopes = (2.0 ** (-ALIBI_MAX_EXP * j / (ATT_SLOTS * N_DIL))).reshape(N_DIL, ATT_SLOTS)
    qi = np.arange(ATT_BLOCK)[:, None] + ATT_BLOCK
    ki = np.arange(2 * ATT_BLOCK)[None, :]
    dist = qi - ki
    valid = (dist >= 0) & (dist <= ATT_BLOCK)
    tab = np.zeros((ATT_SLOTS // 2, N_DIL * 4, ATT_BLOCK, 2 * ATT_BLOCK), np.float32)
    for hp in range(ATT_SLOTS // 2):
        for g, dil in enumerate(DILATIONS):
            for h in range(2):
                slope = np.float32(slopes[g, 2 * hp + h])
                bias = -(slope * (dil * dist).astype(np.float32))
                tab[hp, g * 4 + h * 2 + 0] = np.where(valid, bias, NEG)
                tab[hp, g * 4 + h * 2 + 1] = np.where(valid & (ki >= ATT_BLOCK), bias, NEG)
    return tab


def _att_kernel(q0_ref, q1_ref, q2_ref, kc_ref, kp_ref, vc_ref, vp_ref, bias_ref, o_ref,
                qs, ks0, ks1, ks2, vs0, vs1, vs2, og, lg, onat, lnat, tmp):
    i = pl.program_id(2)
    q_refs = (q0_ref, q1_ref, q2_ref)
    k_scr = (ks0, ks1, ks2)
    v_scr = (vs0, vs1, vs2)
    scale = HEAD_DIM ** -0.5

    for g, dil in enumerate(DILATIONS[:2]):
        lt = ATT_TILE // dil
        cs = ATT_BLOCK + lt
        for r in range(dil):
            cur = pl.ds(r, lt, stride=dil) if dil > 1 else pl.ds(0, lt)
            prv = (pl.ds(ATT_TILE - ATT_BLOCK * dil + r, ATT_BLOCK, stride=dil) if dil > 1
                   else pl.ds(ATT_TILE - ATT_BLOCK, ATT_BLOCK))
            qs[g, r * lt:(r + 1) * lt, :] = (q_refs[g][cur, :] * scale).astype(BF16)
            k_scr[g][r * cs + ATT_BLOCK:(r + 1) * cs, :] = kc_ref[cur, :].astype(BF16)
            k_scr[g][r * cs:r * cs + ATT_BLOCK, :] = kp_ref[prv, :].astype(BF16)
            v_scr[g][r * cs + ATT_BLOCK:(r + 1) * cs, :] = vc_ref[cur, :].astype(BF16)
            v_scr[g][r * cs:r * cs + ATT_BLOCK, :] = vp_ref[prv, :].astype(BF16)

    sub, quarter = DILATIONS[1], ATT_TILE // DILATIONS[1]

    def classes16(src_ref, put, mul=None):
        for r4 in range(sub):
            tmp[r4 * quarter:(r4 + 1) * quarter, :] = src_ref[pl.ds(r4, quarter, stride=sub), :]
        for r in range(DILATIONS[2]):
            v = tmp[pl.ds((r % sub) * quarter + r // sub, ATT_BLOCK, stride=sub), :]
            put(r, (v if mul is None else v * mul).astype(BF16))

    cs16 = 2 * ATT_BLOCK

    def put_q(r, v):
        qs[2, r * ATT_BLOCK:(r + 1) * ATT_BLOCK, :] = v

    def put_at(scr, off):
        def put(r, v):
            scr[r * cs16 + off:r * cs16 + off + ATT_BLOCK, :] = v
        return put

    classes16(q2_ref, put_q, scale)
    classes16(kc_ref, put_at(ks2, ATT_BLOCK))
    classes16(kp_ref, put_at(ks2, 0))
    classes16(vc_ref, put_at(vs2, ATT_BLOCK))
    classes16(vp_ref, put_at(vs2, 0))

    lane = lax.broadcasted_iota(I32, (ATT_BLOCK, LANES), 1)
    head0 = lane < HEAD_DIM

    for g, dil in enumerate(DILATIONS):
        nj = ATT_TILE // dil // ATT_BLOCK

        def block(t, carry, g=g, nj=nj):
            r = t // nj
            j = t - r * nj
            qoff = pl.multiple_of(t * ATT_BLOCK, ATT_BLOCK)
            koff = pl.multiple_of((r * (nj + 1) + j) * ATT_BLOCK, ATT_BLOCK)
            first = jnp.logical_and(i == 0, j == 0).astype(I32)
            qb = qs[g, pl.ds(qoff, ATT_BLOCK), :]
            kb = k_scr[g][pl.ds(koff, 2 * ATT_BLOCK), :]
            vb = v_scr[g][pl.ds(koff, 2 * ATT_BLOCK), :]
            outs, lses = [], []
            for h in range(2):
                qh = jnp.where(head0 if h == 0 else jnp.logical_not(head0), qb, jnp.zeros_like(qb))
                s = lax.dot_general(qh, kb, (((1,), (1,)), ((), ())), preferred_element_type=F32)
                s = s + bias_ref[g * 4 + h * 2 + first]
                m = jnp.max(s, axis=-1, keepdims=True)
                p = jnp.exp(s - m)
                den = jnp.sum(p, axis=-1, keepdims=True)
                o = jnp.dot(p.astype(BF16), vb, preferred_element_type=F32) / den
                outs.append(o)
                lses.append(jnp.broadcast_to(m + jnp.log(den), (ATT_BLOCK, LANES)))
            og[g, pl.ds(qoff, ATT_BLOCK), :] = jnp.where(head0, outs[0], outs[1])
            lg[g, pl.ds(qoff, ATT_BLOCK), :] = jnp.where(head0, lses[0], lses[1])
            return carry

        lax.fori_loop(0, ATT_TILE // ATT_BLOCK, block, 0, unroll=ATT_UNROLL)

    for r in range(sub):
        onat[0, pl.ds(r, quarter, stride=sub), :] = og[1, r * quarter:(r + 1) * quarter, :]
        lnat[0, pl.ds(r, quarter, stride=sub), :] = lg[1, r * quarter:(r + 1) * quarter, :]
    for src, dst in ((og, onat), (lg, lnat)):
        for r in range(DILATIONS[2]):
            tmp[pl.ds((r % sub) * quarter + r // sub, ATT_BLOCK, stride=sub), :] = (
                src[2, r * ATT_BLOCK:(r + 1) * ATT_BLOCK, :])
        for r4 in range(sub):
            dst[1, pl.ds(r4, quarter, stride=sub), :] = tmp[r4 * quarter:(r4 + 1) * quarter, :]

    rc = 256

    def merge(c, carry):
        rows = pl.ds(pl.multiple_of(c * rc, rc), rc)
        l0, l1, l2 = lg[0, rows, :], lnat[0, rows, :], lnat[1, rows, :]
        mx = jnp.maximum(jnp.maximum(l0, l1), l2)
        w0, w1, w2 = jnp.exp(l0 - mx), jnp.exp(l1 - mx), jnp.exp(l2 - mx)
        tot = w0 + w1 + w2
        o = (w0 / tot) * og[0, rows, :] + (w1 / tot) * onat[0, rows, :] + (w2 / tot) * onat[1, rows, :]
        o_ref[rows, :] = o.astype(o_ref.dtype)
        return carry

    lax.fori_loop(0, ATT_TILE // rc, merge, 0)


def _attention(proj, bias_tab, batch, seq):
    t = proj.shape[0]
    nt = seq // ATT_TILE
    cb = lambda off: off // LANES

    def cur(col):
        return pl.BlockSpec((ATT_TILE, LANES), lambda hp, b, i, col=col: (b * nt + i, col + hp))

    def prev(col):
        return pl.BlockSpec((ATT_TILE, LANES),
                            lambda hp, b, i, col=col: (b * nt + jnp.maximum(i - 1, 0), col + hp))

    npairs = ATT_SLOTS // 2
    kv_rows = [ATT_TILE + ATT_BLOCK * d for d in DILATIONS]
    return pl.pallas_call(
        _att_kernel,
        grid=(npairs, batch, nt),
        in_specs=[cur(cb(OFF_QA)), cur(cb(OFF_QA) + npairs), cur(cb(OFF_QA) + 2 * npairs),
                  cur(cb(OFF_KA)), prev(cb(OFF_KA)), cur(cb(OFF_VA)), prev(cb(OFF_VA)),
                  pl.BlockSpec((None, N_DIL * 4, ATT_BLOCK, 2 * ATT_BLOCK), lambda hp, b, i: (hp, 0, 0, 0))],
        out_specs=pl.BlockSpec((ATT_TILE, LANES), lambda hp, b, i: (b * nt + i, hp)),
        out_shape=jax.ShapeDtypeStruct((t, ATT_KV), BF16),
        scratch_shapes=([pltpu.VMEM((N_DIL, ATT_TILE, LANES), BF16)]
                        + [pltpu.VMEM((n, LANES), BF16) for n in kv_rows]
                        + [pltpu.VMEM((n, LANES), BF16) for n in kv_rows]
                        + [pltpu.VMEM((N_DIL, ATT_TILE, LANES), F32)] * 2
                        + [pltpu.VMEM((N_DIL - 1, ATT_TILE, LANES), F32)] * 2
                        + [pltpu.VMEM((ATT_TILE, LANES), F32)]),
        compiler_params=_cparams(("arbitrary", "arbitrary", "arbitrary")),
    )(proj, proj, proj, proj, proj, proj, proj, bias_tab)


CONV_TILE = 256
CONV_HALO = 32
CONV_ROWS = 32


def _conv_kernel(a_ref, gt_ref, ha_ref, hg_ref, w_ref, b_ref, lng_ref, lnb_ref, o_ref, ybuf, win):
    i = pl.program_id(1)
    halo = ha_ref[...] * _sigmoid(hg_ref[...])
    ybuf[0:CONV_HALO, :] = jnp.where(i > 0, halo, jnp.zeros_like(halo))
    ybuf[CONV_HALO:, :] = a_ref[...] * _sigmoid(gt_ref[...])
    first = CONV_HALO - (CONV_K - 1)
    for c in range(CONV_TILE // CONV_ROWS):
        for j in range(SUBLANES):
            span = ((CONV_K - 1 - j) // SUBLANES) * SUBLANES + CONV_ROWS
            lo = c * CONV_ROWS + first + j
            win[j, 0:span, :] = ybuf[lo:lo + span, :]
        acc = jnp.broadcast_to(b_ref[...], (CONV_ROWS, CONV_CH))
        for k in range(CONV_K):
            j, a = k % SUBLANES, k // SUBLANES
            acc = acc + w_ref[k:k + 1, :] * win[j, a * SUBLANES:a * SUBLANES + CONV_ROWS, :]
        mu = jnp.mean(acc, axis=-1, keepdims=True)
        cen = acc - mu
        var = jnp.mean(cen * cen, axis=-1, keepdims=True)
        y = (cen * lax.rsqrt(var + EPS)) * lng_ref[...] + lnb_ref[...]
        o_ref[c * CONV_ROWS:(c + 1) * CONV_ROWS, :] = (y * _sigmoid(y)).astype(o_ref.dtype)


def _conv_module(proj, w, b, ln_g, ln_b, batch, seq):
    t = proj.shape[0]
    nt = seq // CONV_TILE
    hb = CONV_TILE // CONV_HALO
    ca, cg = OFF_CA // CONV_CH, OFF_CG // CONV_CH
    cur = lambda col: pl.BlockSpec((CONV_TILE, CONV_CH), lambda bb, i, col=col: (bb * nt + i, col))
    halo = lambda col: pl.BlockSpec(
        (CONV_HALO, CONV_CH), lambda bb, i, col=col: (jnp.maximum((bb * nt + i) * hb - 1, 0), col))
    vec = pl.BlockSpec((1, CONV_CH), lambda bb, i: (0, 0))
    return pl.pallas_call(
        _conv_kernel,
        grid=(batch, nt),
        in_specs=[cur(ca), cur(cg), halo(ca), halo(cg),
                  pl.BlockSpec((CONV_HALO, CONV_CH), lambda bb, i: (0, 0)), vec, vec, vec],
        out_specs=pl.BlockSpec((CONV_TILE, CONV_CH), lambda bb, i: (bb * nt + i, 0)),
        out_shape=jax.ShapeDtypeStruct((t, CONV_CH), BF16),
        scratch_shapes=[pltpu.VMEM((CONV_HALO + CONV_TILE, CONV_CH), F32),
                        pltpu.VMEM((SUBLANES, CONV_HALO + CONV_ROWS, CONV_CH), F32)],
        compiler_params=_cparams(("arbitrary", "arbitrary")),
    )(proj, proj, proj, proj, w, b, ln_g, ln_b)


def _mlstm_kernel(q_ref, k_ref, v_ref, om_ref, gate_ref, cw_ref, cb_ref, gb_ref, o_ref,
                  ext, ct_s, n_s, m_s, *, batch):
    c = pl.program_id(0)
    lc, dh, nh = MLSTM_CHUNK, MLSTM_DH, MLSTM_HEADS
    assert lc == dh == LANES
    pad = SUBLANES

    @pl.when(c == 0)
    def _():
        ext[:, 0:pad, :] = jnp.zeros((batch, pad, 2 * MLSTM_W), F32)
        ct_s[...] = jnp.zeros_like(ct_s)
        n_s[...] = jnp.zeros_like(n_s)
        m_s[...] = jnp.zeros_like(m_s)

    @pl.when(c > 0)
    def _():
        ext[:, 0:pad, :] = ext[:, lc:lc + pad, :]

    ext[:, pad:, 0:MLSTM_W] = q_ref[...]
    ext[:, pad:, MLSTM_W:] = k_ref[...]

    row = lax.broadcasted_iota(I32, (lc, lc), 0)
    col = lax.broadcasted_iota(I32, (lc, lc), 1)
    causal = row >= col
    tril = causal.astype(F32)
    is_f = jnp.logical_and(col >= nh, col < 2 * nh)

    for b in range(batch):
        acc = jnp.broadcast_to(cb_ref[...], (lc, 2 * MLSTM_W))
        for k in range(MLSTM_CONV_K):
            lo = pad - (MLSTM_CONV_K - 1) + k
            acc = acc + cw_ref[k:k + 1, :] * ext[b, lo:lo + lc, :]
        qk = acc * _sigmoid(acc)

        gpre = gate_ref[b] + gb_ref[...]
        lf = jnp.where(is_f, _log_sigmoid(gpre), jnp.zeros_like(gpre))
        bcum = jnp.dot(tril, lf, precision=HIGHEST, preferred_element_type=F32)
        gpre_t = gpre.T
        bcum_t = bcum.T

        for h in range(nh):
            qh = qk[:, h * dh:(h + 1) * dh]
            kh = qk[:, MLSTM_W + h * dh:MLSTM_W + (h + 1) * dh] * (dh ** -0.5)
            vh = v_ref[b, :, h * dh:(h + 1) * dh]
            qb, kb, vb = qh.astype(BF16), kh.astype(BF16), vh.astype(BF16)
            bq = jnp.broadcast_to(bcum[:, nh + h:nh + h + 1], (lc, lc))
            br = bcum_t[nh + h:nh + h + 1, :]
            igr = gpre_t[h:h + 1, :]
            igc = jnp.broadcast_to(gpre[:, h:h + 1], (lc, lc))
            m_prev = m_s[b, h]
            ct = ct_s[b, h]
            n_row = n_s[b, h]

            dmat = jnp.where(causal, (bq - br) + igr, NEG)
            a_inter = bq + m_prev
            m_t = jnp.maximum(a_inter, jnp.max(dmat, axis=-1, keepdims=True))
            w_intra = jnp.exp(dmat - m_t)
            sc = lax.dot_general(qb, kb, (((1,), (1,)), ((), ())), preferred_element_type=F32) * w_intra
            s_inter = jnp.exp(a_inter - m_t)
            num = (jnp.dot(sc.astype(BF16), vb, preferred_element_type=F32)
                   + s_inter * jnp.dot(qb, ct.astype(BF16), preferred_element_type=F32))
            den = (jnp.sum(sc, axis=-1, keepdims=True)
                   + s_inter * jnp.sum(qh * n_row, axis=-1, keepdims=True))
            hout = num / jnp.maximum(jnp.abs(den), jnp.exp(-m_t))
            og = _sigmoid(om_ref[b, :, h * dh:(h + 1) * dh])
            o_ref[b, :, h * dh:(h + 1) * dh] = (og * hout).astype(o_ref.dtype)

            gtot = bq[lc - 1:lc, :]
            wst = (gtot - bq) + igc
            m_loc = jnp.max(wst, axis=0, keepdims=True)
            ek = jnp.exp(wst - m_loc) * kh
            c_loc = jnp.dot(ek.T.astype(BF16), vb, preferred_element_type=F32)
            n_loc = jnp.sum(ek, axis=0, keepdims=True)
            m_new = jnp.maximum(gtot + m_prev, m_loc)
            fa = jnp.exp(gtot + m_prev - m_new)
            fb = jnp.exp(m_loc - m_new)
            ct_s[b, h] = fa * ct + fb * c_loc
            n_s[b, h] = fa * n_row + fb * n_loc
            m_s[b, h] = m_new


def _mlstm(proj3, cw, cb, gb, batch, seq):
    nc = seq // MLSTM_CHUNK
    wide = lambda off: pl.BlockSpec((batch, MLSTM_CHUNK, MLSTM_W), lambda c, off=off: (0, c, off // MLSTM_W))
    full = lambda shape: pl.BlockSpec(shape, lambda c: (0,) * len(shape))
    return pl.pallas_call(
        functools.partial(_mlstm_kernel, batch=batch),
        grid=(nc,),
        in_specs=[wide(OFF_QM), wide(OFF_KM), wide(OFF_VM), wide(OFF_OM),
                  pl.BlockSpec((batch, MLSTM_CHUNK, LANES), lambda c: (0, c, OFF_IF // LANES)),
                  full((SUBLANES, 2 * MLSTM_W)), full((1, 2 * MLSTM_W)), full((1, LANES))],
        out_specs=pl.BlockSpec((batch, MLSTM_CHUNK, MLSTM_W), lambda c: (0, c, 0)),
        out_shape=jax.ShapeDtypeStruct((batch, seq, MLSTM_W), BF16),
        scratch_shapes=[pltpu.VMEM((batch, MLSTM_CHUNK + SUBLANES, 2 * MLSTM_W), F32),
                        pltpu.VMEM((batch, MLSTM_HEADS, MLSTM_DH, MLSTM_DH), F32),
                        pltpu.VMEM((batch, MLSTM_HEADS, 1, MLSTM_DH), F32),
                        pltpu.VMEM((batch, MLSTM_HEADS, 1, LANES), F32)],
        compiler_params=_cparams(("arbitrary",)),
    )(proj3, proj3, proj3, proj3, proj3, cw, cb, gb)


MERGE_ROWS = 256


def _merge_kernel(x_ref, gate_ref, ya_ref, yc_ref, ym_ref, wa_ref, wc_ref, wm_ref, wo_ref,
                  gf_ref, wr_ref, br_ref, xo_ref, h2_ref, lg_ref, *, tm):
    d = D_MODEL
    for r0 in range(0, tm, MERGE_ROWS):
        rows = slice(r0, r0 + MERGE_ROWS)
        branch = lambda y_ref, w_ref: jnp.dot(y_ref[rows, :], w_ref[...], preferred_element_type=F32)
        merged = (_sigmoid(gate_ref[rows, 0:d]) * branch(ya_ref, wa_ref)
                  + _sigmoid(gate_ref[rows, d:2 * d]) * branch(yc_ref, wc_ref)
                  + _sigmoid(gate_ref[rows, 2 * d:3 * d]) * branch(ym_ref, wm_ref))
        xn = x_ref[rows, :] + jnp.dot(merged.astype(BF16), wo_ref[...], preferred_element_type=F32)
        xo_ref[rows, :] = xn
        h2 = _rms(xn, gf_ref[...])
        lg_ref[rows, :] = jnp.dot(h2.astype(BF16), wr_ref[...], preferred_element_type=F32) + br_ref[...]
        for s in range(ROW_TILES):
            h2_ref[pl.ds(r0 * ROW_TILES + s, MERGE_ROWS, stride=ROW_TILES), :] = h2[:, s * LANES:(s + 1) * LANES]


def _merge(x2, proj, y_att, y_conv, y_m, wa, wc, wm, wo, g_ffn, w_r, b_r, tm=2 * MERGE_ROWS):
    t = x2.shape[0]
    rows = lambda n: pl.BlockSpec((tm, n), lambda i: (i, 0))
    full = lambda a: pl.BlockSpec(a.shape, lambda i: (0, 0))
    return pl.pallas_call(
        functools.partial(_merge_kernel, tm=tm),
        grid=(t // tm,),
        in_specs=[rows(D_MODEL), rows(3 * D_MODEL), rows(ATT_KV), rows(CONV_CH), rows(MLSTM_W),
                  full(wa), full(wc), full(wm), full(wo), full(g_ffn), full(w_r), full(b_r)],
        out_specs=[rows(D_MODEL), pl.BlockSpec((tm * ROW_TILES, LANES), lambda i: (i, 0)), rows(LANES)],
        out_shape=[jax.ShapeDtypeStruct((t, D_MODEL), F32),
                   jax.ShapeDtypeStruct((t * ROW_TILES, LANES), F32),
                   jax.ShapeDtypeStruct((t, LANES), F32)],
        compiler_params=_cparams(("arbitrary",)),
    )(x2, proj, y_att, y_conv, y_m, wa, wc, wm, wo, g_ffn, w_r, b_r)


ROUTER_TILE = 512


def _router_kernel(lg_ref, idx_ref, wt_ref, cnt_ref, run):
    n = ROUTER_TILE
    ng, ne = N_GROUPS_MOE, EXPERTS_PER_GROUP

    @pl.when(pl.program_id(0) == 0)
    def _():
        run[...] = jnp.zeros_like(run)

    lt = lg_ref[...].T
    lgrp = lt[0:ng, :]
    gmax = jnp.max(lgrp, axis=0, keepdims=True)
    grow = lax.broadcasted_iota(I32, (ng, n), 0)
    g_sel = jnp.min(jnp.where(lgrp == gmax, grow, ng), axis=0, keepdims=True)
    p_g = 1.0 / jnp.sum(jnp.exp(lgrp - gmax), axis=0, keepdims=True)

    le_sel = jnp.zeros((ne, n), F32)
    for g in range(ng):
        le_sel = jnp.where(g_sel == g, lt[(g + 1) * ne:(g + 2) * ne, :], le_sel)
    erow = lax.broadcasted_iota(I32, (ne, n), 0)
    v1 = jnp.max(le_sel, axis=0, keepdims=True)
    i1 = jnp.min(jnp.where(le_sel == v1, erow, ne), axis=0, keepdims=True)
    rest = jnp.where(erow == i1, -jnp.inf, le_sel)
    v2 = jnp.max(rest, axis=0, keepdims=True)
    i2 = jnp.min(jnp.where(rest == v2, erow, ne), axis=0, keepdims=True)
    e2 = jnp.exp(v2 - v1)
    w1 = p_g * (1.0 / (1.0 + e2))
    w2 = p_g * (e2 / (1.0 + e2))
    lo = jnp.minimum(i1, i2)
    hi = jnp.maximum(i1, i2)
    pair = ((lo * (2 * ne - 1 - lo)) >> 1) + (hi - lo - 1)
    cls = g_sel * PAIRS_PER_GROUP + pair
    first_is_lo = i1 < i2
    w_lo = jnp.where(first_is_lo, w1, w2)
    w_hi = jnp.where(first_is_lo, w2, w1)

    hit = lax.broadcasted_iota(I32, (LANES, n), 0) == cls
    onehot = jnp.where(hit, 1.0, 0.0)
    r_i = lax.broadcasted_iota(I32, (n, n), 0)
    c_i = lax.broadcasted_iota(I32, (n, n), 1)
    before = jnp.where(r_i < c_i, 1.0, 0.0).astype(BF16)
    cnt = jnp.dot(onehot.astype(BF16), before, preferred_element_type=F32) + run[:, 0:1]
    rank = jnp.sum(jnp.where(hit, cnt, 0.0), axis=0, keepdims=True)
    run[...] = run[...] + jnp.sum(onehot, axis=1, keepdims=True)

    idx_ref[...] = jnp.concatenate([cls, rank.astype(I32), jnp.zeros((SUBLANES - 2, n), I32)], axis=0)
    wt_ref[...] = jnp.concatenate([w_lo, w_hi, jnp.zeros((SUBLANES - 2, n), F32)], axis=0)
    cnt_ref[...] = run[...]


def _router(logits):
    t = logits.shape[0]
    n = ROUTER_TILE
    return pl.pallas_call(
        _router_kernel,
        grid=(t // n,),
        in_specs=[pl.BlockSpec((n, LANES), lambda i: (i, 0))],
        out_specs=[pl.BlockSpec((SUBLANES, n), lambda i: (0, i)),
                   pl.BlockSpec((SUBLANES, n), lambda i: (0, i)),
                   pl.BlockSpec((LANES, LANES), lambda i: (0, 0))],
        out_shape=[jax.ShapeDtypeStruct((SUBLANES, t), I32),
                   jax.ShapeDtypeStruct((SUBLANES, t), F32),
                   jax.ShapeDtypeStruct((LANES, LANES), F32)],
        scratch_shapes=[pltpu.VMEM((LANES, LANES), F32)],
        compiler_params=_cparams(("arbitrary",)),
    )(logits)


DISPATCH_TILE = 512


def _dispatch_kernel(dest_ref, x_ref, xs_in_ref, xs_ref, sem):
    del xs_in_ref
    n = DISPATCH_TILE

    def row_copy(t):
        d = pl.multiple_of(dest_ref[0, 0, t], ROW_TILES)
        src = x_ref.at[pl.ds(pl.multiple_of(t * ROW_TILES, ROW_TILES), ROW_TILES), :]
        return pltpu.make_async_copy(src, xs_ref.at[pl.ds(d, ROW_TILES), :], sem)

    def issue(t, carry):
        row_copy(t).start()
        return carry

    lax.fori_loop(0, n, issue, 0, unroll=8)

    def drain(t, carry):
        row_copy(t).wait()
        return carry

    lax.fori_loop(0, n, drain, 0, unroll=8)


def _dispatch(h2t, dest_tiles, p_len):
    t = h2t.shape[0] // ROW_TILES
    n = DISPATCH_TILE
    xs0 = jnp.zeros((p_len * ROW_TILES, LANES), F32)
    return pl.pallas_call(
        _dispatch_kernel,
        grid=(t // n,),
        in_specs=[pl.BlockSpec((1, 1, n), lambda i: (i, 0, 0), memory_space=pltpu.SMEM),
                  pl.BlockSpec((n * ROW_TILES, LANES), lambda i: (i, 0)),
                  pl.BlockSpec(memory_space=pl.ANY)],
        out_specs=pl.BlockSpec(memory_space=pl.ANY),
        out_shape=jax.ShapeDtypeStruct((p_len * ROW_TILES, LANES), F32),
        scratch_shapes=[pltpu.SemaphoreType.DMA(())],
        input_output_aliases={2: 0},
        compiler_params=_cparams(("arbitrary",)),
    )(dest_tiles, h2t, xs0)


def _expert_kernel(lo_ref, hi_ref, nu_ref, xs_ref, *refs):
    w_refs, y_ref = refs[:-1], refs[-1]
    first = pl.program_id(0) * EXPERT_BLOCKS
    used = jnp.clip(nu_ref[0] - first, 0, EXPERT_BLOCKS)

    def run(sub, nblk):
        n = nblk * MOE_BLOCK
        rows_in = sub * MOE_BLOCK * ROW_TILES
        rows_out = sub * MOE_BLOCK * PAIR_TILES
        xb = jnp.concatenate(
            [xs_ref[pl.ds(rows_in + s, n, stride=ROW_TILES), :] for s in range(ROW_TILES)], axis=-1).astype(BF16)
        for half in range(2):
            wg, wu, wd = w_refs[sub * 6 + half * 3:sub * 6 + half * 3 + 3]
            hg = jnp.dot(xb, wg[...], preferred_element_type=F32)
            hu = jnp.dot(xb, wu[...], preferred_element_type=F32)
            hb = (hg * _sigmoid(hg)) * hu
            y = jnp.dot(hb.astype(BF16), wd[...], preferred_element_type=F32)
            for s in range(ROW_TILES):
                y_ref[pl.ds(rows_out + half * ROW_TILES + s, n, stride=PAIR_TILES), :] = (
                    y[:, s * LANES:(s + 1) * LANES])

    def clear(sub):
        n = MOE_BLOCK * PAIR_TILES
        y_ref[sub * n:(sub + 1) * n, :] = jnp.zeros((n, LANES), F32)

    same = jnp.logical_and(lo_ref[first] == lo_ref[first + 1], hi_ref[first] == hi_ref[first + 1])
    both = used == EXPERT_BLOCKS

    @pl.when(jnp.logical_and(both, same))
    def _():
        run(0, 2)

    @pl.when(jnp.logical_and(both, jnp.logical_not(same)))
    def _():
        run(0, 1)
        run(1, 1)

    @pl.when(used == 1)
    def _():
        run(0, 1)
        clear(1)

    @pl.when(used == 0)
    def _():
        clear(0)
        clear(1)


def _experts(blk_lo, blk_hi, n_used, xs, wg, wu, wd):
    nb = blk_lo.shape[0]
    last = lambda b, nu: jnp.minimum(b, nu[0] - 1)
    xrows = pl.BlockSpec((EXPERT_BLOCKS * MOE_BLOCK * ROW_TILES, LANES), lambda i, lo, hi, nu: (i, 0))
    yrows = pl.BlockSpec((EXPERT_BLOCKS * MOE_BLOCK * PAIR_TILES, LANES), lambda i, lo, hi, nu: (i, 0))
    up, down = (D_MODEL, D_EXPERT), (D_EXPERT, D_MODEL)

    def weight(shape, sub, high):
        def index(i, lo, hi, nu):
            b = last(i * EXPERT_BLOCKS + sub, nu)
            return ((hi if high else lo)[b], 0, 0)
        return pl.BlockSpec((None,) + shape, index)

    w_specs, w_args = [], []
    for sub in range(EXPERT_BLOCKS):
        for high in (False, True):
            w_specs += [weight(up, sub, high), weight(up, sub, high), weight(down, sub, high)]
            w_args += [wg, wu, wd]
    return pl.pallas_call(
        _expert_kernel,
        grid_spec=pltpu.PrefetchScalarGridSpec(
            num_scalar_prefetch=3,
            grid=(nb // EXPERT_BLOCKS,),
            in_specs=[xrows] + w_specs,
            out_specs=yrows),
        out_shape=jax.ShapeDtypeStruct((xs.shape[0] * 2, LANES), F32),
        compiler_params=_cparams(("arbitrary",)),
    )(blk_lo, blk_hi, n_used, xs, *w_args)


COMBINE_TILE = 512
COMBINE_ROWS = 256


def _combine_kernel(dcur_ref, dnxt_ref, x_ref, wt_ref, p_ref, y_hbm, gp_ref, wpg_ref, wpp_ref, gfin_ref,
                    o_ref, ybuf, sems, *, final):
    n = COMBINE_TILE
    i = pl.program_id(0)
    slot = i % 2

    def row_copy(dref, sl, t):
        d = pl.multiple_of(dref[0, 0, t], PAIR_TILES)
        dst = ybuf.at[sl, pl.ds(pl.multiple_of(t * PAIR_TILES, PAIR_TILES), PAIR_TILES), :]
        return pltpu.make_async_copy(y_hbm.at[pl.ds(d, PAIR_TILES), :], dst, sems.at[sl])

    def issue(dref, sl):
        def body(t, carry):
            row_copy(dref, sl, t).start()
            return carry
        lax.fori_loop(0, n, body, 0, unroll=8)

    @pl.when(i == 0)
    def _():
        issue(dcur_ref, slot)

    @pl.when(i + 1 < pl.num_programs(0))
    def _():
        issue(dnxt_ref, 1 - slot)

    pes = [jnp.dot(p_ref[r0:r0 + COMBINE_ROWS, :].astype(BF16), wpp_ref[...], preferred_element_type=F32)
           for r0 in range(0, n, COMBINE_ROWS)]

    def drain(t, carry):
        row_copy(dcur_ref, slot, t).wait()
        return carry

    lax.fori_loop(0, n, drain, 0, unroll=8)

    def rows_of(half, r0):
        return jnp.concatenate(
            [ybuf[slot, pl.ds(r0 * PAIR_TILES + half * ROW_TILES + s, COMBINE_ROWS, stride=PAIR_TILES), :]
             for s in range(ROW_TILES)], axis=-1)

    for part, r0 in enumerate(range(0, n, COMBINE_ROWS)):
        rows = slice(r0, r0 + COMBINE_ROWS)
        xn = x_ref[rows, :] + (wt_ref[rows, 0:1] * rows_of(0, r0) + wt_ref[rows, 1:2] * rows_of(1, r0))
        hp = _rms(xn, gp_ref[...])
        gate = _sigmoid(jnp.dot(hp.astype(BF16), wpg_ref[...], preferred_element_type=F32))
        xo = xn + gate * pes[part]
        if final:
            xo = _rms(xo, gfin_ref[...])
        o_ref[rows, :] = xo


def _combine(dest_tiles, x2, wts, p2, y, g_ple, wpg, wpp, g_final, final):
    t = x2.shape[0]
    n = COMBINE_TILE
    nt = t // n
    rows = lambda w: pl.BlockSpec((n, w), lambda i: (i, 0))
    full = lambda a: pl.BlockSpec(a.shape, lambda i: (0, 0))
    return pl.pallas_call(
        functools.partial(_combine_kernel, final=final),
        grid=(nt,),
        in_specs=[pl.BlockSpec((1, 1, n), lambda i: (i, 0, 0), memory_space=pltpu.SMEM),
                  pl.BlockSpec((1, 1, n), lambda i: (jnp.minimum(i + 1, nt - 1), 0, 0), memory_space=pltpu.SMEM),
                  rows(D_MODEL), rows(2), rows(PLE_DIM),
                  pl.BlockSpec(memory_space=pl.ANY),
                  full(g_ple), full(wpg), full(wpp), full(g_final)],
        out_specs=rows(D_MODEL),
        out_shape=jax.ShapeDtypeStruct((t, D_MODEL), F32),
        scratch_shapes=[pltpu.VMEM((2, n * PAIR_TILES, LANES), F32), pltpu.SemaphoreType.DMA((2,))],
        compiler_params=_cparams(("arbitrary",)),
    )(dest_tiles, dest_tiles, x2, wts, p2, y, g_ple, wpg, wpp, g_final)


CAST_BLOCK_BYTES = 2 * 1024 * 1024


def _cast_kernel(w_ref, o_ref):
    o_ref[...] = w_ref[...].astype(o_ref.dtype)


def _layer_bf16(w, layer):
    cols = w.shape[-1]
    rows = int(np.prod(w.shape[1:-1]))
    rb = min(rows, CAST_BLOCK_BYTES // (4 * cols))
    out = pl.pallas_call(
        _cast_kernel,
        grid=(rows // rb,),
        in_specs=[pl.BlockSpec((None, rb, cols), lambda i: (layer, i, 0))],
        out_specs=pl.BlockSpec((rb, cols), lambda i: (i, 0)),
        out_shape=jax.ShapeDtypeStruct((rows, cols), BF16),
        compiler_params=_cparams(("arbitrary",)),
    )(w.reshape(w.shape[0], rows, cols))
    return out.reshape(w.shape[1:])


def _class_experts():
    lo, hi = [], []
    for g in range(N_GROUPS_MOE):
        for a in range(EXPERTS_PER_GROUP):
            for b in range(a + 1, EXPERTS_PER_GROUP):
                lo.append(g * EXPERTS_PER_GROUP + a)
                hi.append(g * EXPERTS_PER_GROUP + b)
    return np.asarray(lo, np.int32), np.asarray(hi, np.int32)


def _moe_plan(idx, cnt, t):
    counts = cnt[:N_CLASSES, 0].astype(I32)
    padded = (counts + MOE_BLOCK - 1) // MOE_BLOCK * MOE_BLOCK
    ends_p = jnp.cumsum(padded)
    pstart = ends_p - padded
    classes = jnp.arange(N_CLASSES, dtype=I32)
    dest = jnp.sum(jnp.where(idx[0][:, None] == classes, pstart, 0), axis=-1) + idx[1]
    p_len = t + N_CLASSES * MOE_BLOCK
    nb = p_len // MOE_BLOCK
    blk_start = jnp.arange(nb, dtype=I32) * MOE_BLOCK
    blk_cls = jnp.minimum(jnp.sum((ends_p[None, :] <= blk_start[:, None]).astype(I32), axis=1), N_CLASSES - 1)
    lo_tab, hi_tab = _class_experts()
    is_c = blk_cls[:, None] == classes
    blk_lo = jnp.sum(jnp.where(is_c, lo_tab, 0), axis=1).astype(I32)
    blk_hi = jnp.sum(jnp.where(is_c, hi_tab, 0), axis=1).astype(I32)
    n_used = (ends_p[-1:] // MOE_BLOCK).astype(I32)
    return dest, blk_lo, blk_hi, n_used, p_len


def kernel(x, p, g_mix, w_in, conv_w, conv_b, conv_ln_g, conv_ln_b, mlstm_conv_w, mlstm_conv_b,
           mlstm_i_bias, mlstm_f_bias, w_br_att, w_br_conv, w_br_mlstm, w_out, g_ffn,
           w_router_g, b_router_g, w_router_e, b_router_e, w_exp_gate, w_exp_up, w_exp_down,
           g_ple, w_ple_gate, w_ple_proj, g_final):
    batch, seq, d = x.shape
    depth = w_in.shape[0]
    t = batch * seq
    bias_tab = jnp.asarray(_att_bias_table())
    x2 = x.reshape(t, d)
    nr = N_GROUPS_MOE + N_EXPERTS
    for i in range(depth):
        w = w_in[i]
        wp = jnp.concatenate([w[:, ORIG_SMALL + 2 * MLSTM_HEADS:], w[:, :ORIG_SMALL],
                              w[:, ORIG_SMALL:ORIG_SMALL + 2 * MLSTM_HEADS],
                              jnp.zeros((d, LANES - 2 * MLSTM_HEADS), F32)], axis=1).astype(BF16)
        proj = _inproj(x2, g_mix[i][None, :], wp)

        y_att = _attention(proj, bias_tab, batch, seq)
        cw = jnp.concatenate([conv_w[i], jnp.zeros((CONV_HALO - CONV_K, CONV_CH), F32)], axis=0)
        y_conv = _conv_module(proj, cw, conv_b[i][None, :], conv_ln_g[i][None, :], conv_ln_b[i][None, :],
                              batch, seq)
        mw = jnp.concatenate([mlstm_conv_w[i], jnp.zeros((SUBLANES - MLSTM_CONV_K, 2 * MLSTM_W), F32)], axis=0)
        gb = jnp.concatenate([mlstm_i_bias[i], mlstm_f_bias[i],
                              jnp.zeros((LANES - 2 * MLSTM_HEADS,), F32)])[None, :]
        y_m = _mlstm(proj.reshape(batch, seq, PROJ_N), mw, mlstm_conv_b[i][None, :], gb, batch, seq)

        gpad = EXPERTS_PER_GROUP - N_GROUPS_MOE
        w_r = jnp.concatenate([w_router_g[i], jnp.zeros((d, gpad), F32), w_router_e[i],
                               jnp.zeros((d, LANES - nr - gpad), F32)], axis=1)
        b_r = jnp.concatenate([b_router_g[i], jnp.zeros((gpad,), F32), b_router_e[i],
                               jnp.zeros((LANES - nr - gpad,), F32)])[None, :]
        x2, h2t, logits = _merge(x2, proj, y_att, y_conv, y_m.reshape(t, MLSTM_W),
                                 w_br_att[i].astype(BF16), w_br_conv[i].astype(BF16),
                                 w_br_mlstm[i].astype(BF16), w_out[i].astype(BF16),
                                 g_ffn[i][None, :], w_r.astype(BF16), b_r)

        idx, wt, cnt = _router(logits)
        dest, blk_lo, blk_hi, n_used, p_len = _moe_plan(idx, cnt, t)
        xs = _dispatch(h2t, (dest * ROW_TILES).reshape(t // DISPATCH_TILE, 1, DISPATCH_TILE), p_len)
        ys = _experts(blk_lo, blk_hi, n_used, xs, _layer_bf16(w_exp_gate, i), _layer_bf16(w_exp_up, i),
                      _layer_bf16(w_exp_down, i))
        x2 = _combine((dest * PAIR_TILES).reshape(t // COMBINE_TILE, 1, COMBINE_TILE), x2, wt[0:2].T,
                      p[i].reshape(t, PLE_DIM), ys, g_ple[i][None, :], w_ple_gate[i].astype(BF16),
                      w_ple_proj[i].astype(BF16), g_final[None, :], final=(i == depth - 1))
    return x2.reshape(batch, seq, d)
```

```python
import functools

import numpy as np
import jax
import jax.numpy as jnp
from jax import lax
from jax.experimental import pallas as pl
from jax.experimental.pallas import tpu as pltpu

F32 = jnp.float32
BF16 = jnp.bfloat16
I32 = jnp.int32
HIGHEST = lax.Precision.HIGHEST

D_MODEL = 1024
HEAD_DIM = 64
ATT_SLOTS = 8
DILATIONS = (1, 4, 16)
N_DIL = 3
ATT_BLOCK = 128
ATT_Q = ATT_SLOTS * N_DIL * HEAD_DIM
ATT_KV = ATT_SLOTS * HEAD_DIM
ALIBI_MAX_EXP = 8.0
CONV_CH = 512
CONV_K = 31
MLSTM_HEADS = 4
MLSTM_DH = 128
MLSTM_W = MLSTM_HEADS * MLSTM_DH
MLSTM_CONV_K = 4
MLSTM_CHUNK = 128
N_GROUPS_MOE = 4
EXPERTS_PER_GROUP = 8
N_EXPERTS = N_GROUPS_MOE * EXPERTS_PER_GROUP
D_EXPERT = 256
MOE_BLOCK = 128
PAIRS_PER_GROUP = EXPERTS_PER_GROUP * (EXPERTS_PER_GROUP - 1) // 2
N_CLASSES = N_GROUPS_MOE * PAIRS_PER_GROUP
PLE_DIM = 256
EPS = 1e-6
NEG = -1e30

LANES = 128
SUBLANES = 8
ROW_TILES = D_MODEL // LANES
PAIR_TILES = 2 * ROW_TILES
EXPERT_BLOCKS = 2
DMA_GROUP = 8
VMEM_LIMIT = 56 * 1024 * 1024

OFF_GATE = 0
OFF_QA = 3 * D_MODEL
OFF_KA = OFF_QA + ATT_Q
OFF_VA = OFF_KA + ATT_KV
OFF_CA = OFF_VA + ATT_KV
OFF_CG = OFF_CA + CONV_CH
OFF_QM = OFF_CG + CONV_CH
OFF_KM = OFF_QM + MLSTM_W
OFF_VM = OFF_KM + MLSTM_W
OFF_OM = OFF_VM + MLSTM_W
OFF_IF = OFF_OM + MLSTM_W
PROJ_N = OFF_IF + LANES
PROJ_TN = PROJ_N // 3
ORIG_SMALL = ATT_Q + 2 * ATT_KV + 2 * CONV_CH + 4 * MLSTM_W

ATT_TILE = ATT_BLOCK * DILATIONS[-1]
ATT_UNROLL = 16


def _cparams(sem, vmem=VMEM_LIMIT):
    return pltpu.CompilerParams(dimension_semantics=sem, vmem_limit_bytes=vmem)


def _rms(x, g):
    r = lax.rsqrt(jnp.mean(x * x, axis=-1, keepdims=True) + EPS)
    return (x * r) * g


def _sigmoid(x):
    return 1.0 / (1.0 + jnp.exp(-x))


def _log_sigmoid(x):
    return jnp.minimum(x, 0.0) - jnp.log(1.0 + jnp.exp(-jnp.abs(x)))


def _inproj_kernel(x_ref, g_ref, w_ref, o_ref):
    h = _rms(x_ref[...], g_ref[...]).astype(BF16)
    for j in range(PROJ_N // PROJ_TN):
        cols = slice(j * PROJ_TN, (j + 1) * PROJ_TN)
        o_ref[:, cols] = jnp.dot(h, w_ref[:, cols], preferred_element_type=F32)


def _inproj(x2, g, w, tm=256):
    t = x2.shape[0]
    return pl.pallas_call(
        _inproj_kernel,
        grid=(t // tm,),
        in_specs=[pl.BlockSpec((tm, D_MODEL), lambda i: (i, 0)),
                  pl.BlockSpec((1, D_MODEL), lambda i: (0, 0)),
                  pl.BlockSpec((D_MODEL, PROJ_N), lambda i: (0, 0), pipeline_mode=pl.Buffered(1))],
        out_specs=pl.BlockSpec((tm, PROJ_N), lambda i: (i, 0)),
        out_shape=jax.ShapeDtypeStruct((t, PROJ_N), F32),
        compiler_params=_cparams(("arbitrary",)),
    )(x2, g, w)


def _att_bias_table():
    j = np.arange(1, ATT_SLOTS * N_DIL + 1, dtype=np.float64)
    slopes = (2.0 ** (-ALIBI_MAX_EXP * j / (ATT_SLOTS * N_DIL))).reshape(N_DIL, ATT_SLOTS)
    qi = np.arange(ATT_BLOCK)[:, None] + ATT_BLOCK
    ki = np.arange(2 * ATT_BLOCK)[None, :]
    dist = qi - ki
    valid = (dist >= 0) & (dist <= ATT_BLOCK)
    tab = np.zeros((ATT_SLOTS // 2, N_DIL * 4, ATT_BLOCK, 2 * ATT_BLOCK), np.float32)
    for hp in range(ATT_SLOTS // 2):
        for g, dil in enumerate(DILATIONS):
            for h in range(2):
                slope = np.float32(slopes[g, 2 * hp + h])
                bias = -(slope * (dil * dist).astype(np.float32))
                tab[hp, g * 4 + h * 2 + 0] = np.where(valid, bias, NEG)
                tab[hp, g * 4 + h * 2 + 1] = np.where(valid & (ki >= ATT_BLOCK), bias, NEG)
    return tab


def _att_kernel(q0_ref, q1_ref, q2_ref, kc_ref, kp_ref, vc_ref, vp_ref, bias_ref, o_ref,
                qs, ks0, ks1, ks2, vs0, vs1, vs2, og, lg, onat, lnat, tmp):
    i = pl.program_id(2)
    q_refs = (q0_ref, q1_ref, q2_ref)
    k_scr = (ks0, ks1, ks2)
    v_scr = (vs0, vs1, vs2)
    scale = HEAD_DIM ** -0.5

    for g, dil in enumerate(DILATIONS[:2]):
        lt = ATT_TILE // dil
        cs = ATT_BLOCK + lt
        for r in range(dil):
            cur = pl.ds(r, lt, stride=dil) if dil > 1 else pl.ds(0, lt)
            prv = (pl.ds(ATT_TILE - ATT_BLOCK * dil + r, ATT_BLOCK, stride=dil) if dil > 1
                   else pl.ds(ATT_TILE - ATT_BLOCK, ATT_BLOCK))
            qs[g, r * lt:(r + 1) * lt, :] = (q_refs[g][cur, :] * scale).astype(BF16)
            k_scr[g][r * cs + ATT_BLOCK:(r + 1) * cs, :] = kc_ref[cur, :].astype(BF16)
            k_scr[g][r * cs:r * cs + ATT_BLOCK, :] = kp_ref[prv, :].astype(BF16)
            v_scr[g][r * cs + ATT_BLOCK:(r + 1) * cs, :] = vc_ref[cur, :].astype(BF16)
            v_scr[g][r * cs:r * cs + ATT_BLOCK, :] = vp_ref[prv, :].astype(BF16)

    sub, quarter = DILATIONS[1], ATT_TILE // DILATIONS[1]

    def classes16(src_ref, put, mul=None):
        for r4 in range(sub):
            tmp[r4 * quarter:(r4 + 1) * quarter, :] = src_ref[pl.ds(r4, quarter, stride=sub), :]
        for r in range(DILATIONS[2]):
            v = tmp[pl.ds((r % sub) * quarter + r // sub, ATT_BLOCK, stride=sub), :]
            put(r, (v if mul is None else v * mul).astype(BF16))

    cs16 = 2 * ATT_BLOCK

    def put_q(r, v):
        qs[2, r * ATT_BLOCK:(r + 1) * ATT_BLOCK, :] = v

    def put_at(scr, off):
        def put(r, v):
            scr[r * cs16 + off:r * cs16 + off + ATT_BLOCK, :] = v
        return put

    classes16(q2_ref, put_q, scale)
    classes16(kc_ref, put_at(ks2, ATT_BLOCK))
    classes16(kp_ref, put_at(ks2, 0))
    classes16(vc_ref, put_at(vs2, ATT_BLOCK))
    classes16(vp_ref, put_at(vs2, 0))

    lane = lax.broadcasted_iota(I32, (ATT_BLOCK, LANES), 1)
    head0 = lane < HEAD_DIM

    for g, dil in enumerate(DILATIONS):
        nj = ATT_TILE // dil // ATT_BLOCK

        def block(t, carry, g=g, nj=nj):
            r = t // nj
            j = t - r * nj
            qoff = pl.multiple_of(t * ATT_BLOCK, ATT_BLOCK)
            koff = pl.multiple_of((r * (nj + 1) + j) * ATT_BLOCK, ATT_BLOCK)
            first = jnp.logical_and(i == 0, j == 0).astype(I32)
            qb = qs[g, pl.ds(qoff, ATT_BLOCK), :]
            kb = k_scr[g][pl.ds(koff, 2 * ATT_BLOCK), :]
            vb = v_scr[g][pl.ds(koff, 2 * ATT_BLOCK), :]
            outs, lses = [], []
            for h in range(2):
                qh = jnp.where(head0 if h == 0 else jnp.logical_not(head0), qb, jnp.zeros_like(qb))
                s = lax.dot_general(qh, kb, (((1,), (1,)), ((), ())), preferred_element_type=F32)
                s = s + bias_ref[g * 4 + h * 2 + first]
                m = jnp.max(s, axis=-1, keepdims=True)
                p = jnp.exp(s - m)
                den = jnp.sum(p, axis=-1, keepdims=True)
                o = jnp.dot(p.astype(BF16), vb, preferred_element_type=F32) / den
                outs.append(o)
                lses.append(jnp.broadcast_to(m + jnp.log(den), (ATT_BLOCK, LANES)))
            og[g, pl.ds(qoff, ATT_BLOCK), :] = jnp.where(head0, outs[0], outs[1])
            lg[g, pl.ds(qoff, ATT_BLOCK), :] = jnp.where(head0, lses[0], lses[1])
            return carry

        lax.fori_loop(0, ATT_TILE // ATT_BLOCK, block, 0, unroll=ATT_UNROLL)

    for r in range(sub):
        onat[0, pl.ds(r, quarter, stride=sub), :] = og[1, r * quarter:(r + 1) * quarter, :]
        lnat[0, pl.ds(r, quarter, stride=sub), :] = lg[1, r * quarter:(r + 1) * quarter, :]
    for src, dst in ((og, onat), (lg, lnat)):
        for r in range(DILATIONS[2]):
            tmp[pl.ds((r % sub) * quarter + r // sub, ATT_BLOCK, stride=sub), :] = (
                src[2, r * ATT_BLOCK:(r + 1) * ATT_BLOCK, :])
        for r4 in range(sub):
            dst[1, pl.ds(r4, quarter, stride=sub), :] = tmp[r4 * quarter:(r4 + 1) * quarter, :]

    rc = 256

    def merge(c, carry):
        rows = pl.ds(pl.multiple_of(c * rc, rc), rc)
        l0, l1, l2 = lg[0, rows, :], lnat[0, rows, :], lnat[1, rows, :]
        mx = jnp.maximum(jnp.maximum(l0, l1), l2)
        w0, w1, w2 = jnp.exp(l0 - mx), jnp.exp(l1 - mx), jnp.exp(l2 - mx)
        tot = w0 + w1 + w2
        o = (w0 / tot) * og[0, rows, :] + (w1 / tot) * onat[0, rows, :] + (w2 / tot) * onat[1, rows, :]
        o_ref[rows, :] = o.astype(o_ref.dtype)
        return carry

    lax.fori_loop(0, ATT_TILE // rc, merge, 0)


def _attention(proj, bias_tab, batch, seq):
    t = proj.shape[0]
    nt = seq // ATT_TILE
    cb = lambda off: off // LANES

    def cur(col):
        return pl.BlockSpec((ATT_TILE, LANES), lambda hp, b, i, col=col: (b * nt + i, col + hp))

    def prev(col):
        return pl.BlockSpec((ATT_TILE, LANES),
                            lambda hp, b, i, col=col: (b * nt + jnp.maximum(i - 1, 0), col + hp))

    npairs = ATT_SLOTS // 2
    kv_rows = [ATT_TILE + ATT_BLOCK * d for d in DILATIONS]
    return pl.pallas_call(
        _att_kernel,
        grid=(npairs, batch, nt),
        in_specs=[cur(cb(OFF_QA)), cur(cb(OFF_QA) + npairs), cur(cb(OFF_QA) + 2 * npairs),
                  cur(cb(OFF_KA)), prev(cb(OFF_KA)), cur(cb(OFF_VA)), prev(cb(OFF_VA)),
                  pl.BlockSpec((None, N_DIL * 4, ATT_BLOCK, 2 * ATT_BLOCK), lambda hp, b, i: (hp, 0, 0, 0))],
        out_specs=pl.BlockSpec((ATT_TILE, LANES), lambda hp, b, i: (b * nt + i, hp)),
        out_shape=jax.ShapeDtypeStruct((t, ATT_KV), BF16),
        scratch_shapes=([pltpu.VMEM((N_DIL, ATT_TILE, LANES), BF16)]
                        + [pltpu.VMEM((n, LANES), BF16) for n in kv_rows]
                        + [pltpu.VMEM((n, LANES), BF16) for n in kv_rows]
                        + [pltpu.VMEM((N_DIL, ATT_TILE, LANES), F32)] * 2
                        + [pltpu.VMEM((N_DIL - 1, ATT_TILE, LANES), F32)] * 2
                        + [pltpu.VMEM((ATT_TILE, LANES), F32)]),
        compiler_params=_cparams(("arbitrary", "arbitrary", "arbitrary")),
    )(proj, proj, proj, proj, proj, proj, proj, bias_tab)


CONV_TILE = 256
CONV_HALO = 32
CONV_ROWS = 32


def _conv_kernel(a_ref, gt_ref, ha_ref, hg_ref, w_ref, b_ref, lng_ref, lnb_ref, o_ref, ybuf, win):
    i = pl.program_id(1)
    halo = ha_ref[...] * _sigmoid(hg_ref[...])
    ybuf[0:CONV_HALO, :] = jnp.where(i > 0, halo, jnp.zeros_like(halo))
    ybuf[CONV_HALO:, :] = a_ref[...] * _sigmoid(gt_ref[...])
    first = CONV_HALO - (CONV_K - 1)
    for c in range(CONV_TILE // CONV_ROWS):
        for j in range(SUBLANES):
            span = ((CONV_K - 1 - j) // SUBLANES) * SUBLANES + CONV_ROWS
            lo = c * CONV_ROWS + first + j
            win[j, 0:span, :] = ybuf[lo:lo + span, :]
        acc = jnp.broadcast_to(b_ref[...], (CONV_ROWS, CONV_CH))
        for k in range(CONV_K):
            j, a = k % SUBLANES, k // SUBLANES
            acc = acc + w_ref[k:k + 1, :] * win[j, a * SUBLANES:a * SUBLANES + CONV_ROWS, :]
        mu = jnp.mean(acc, axis=-1, keepdims=True)
        cen = acc - mu
        var = jnp.mean(cen * cen, axis=-1, keepdims=True)
        y = (cen * lax.rsqrt(var + EPS)) * lng_ref[...] + lnb_ref[...]
        o_ref[c * CONV_ROWS:(c + 1) * CONV_ROWS, :] = (y * _sigmoid(y)).astype(o_ref.dtype)


def _conv_module(proj, w, b, ln_g, ln_b, batch, seq):
    t = proj.shape[0]
    nt = seq // CONV_TILE
    hb = CONV_TILE // CONV_HALO
    ca, cg = OFF_CA // CONV_CH, OFF_CG // CONV_CH
    cur = lambda col: pl.BlockSpec((CONV_TILE, CONV_CH), lambda bb, i, col=col: (bb * nt + i, col))
    halo = lambda col: pl.BlockSpec(
        (CONV_HALO, CONV_CH), lambda bb, i, col=col: (jnp.maximum((bb * nt + i) * hb - 1, 0), col))
    vec = pl.BlockSpec((1, CONV_CH), lambda bb, i: (0, 0))
    return pl.pallas_call(
        _conv_kernel,
        grid=(batch, nt),
        in_specs=[cur(ca), cur(cg), halo(ca), halo(cg),
                  pl.BlockSpec((CONV_HALO, CONV_CH), lambda bb, i: (0, 0)), vec, vec, vec],
        out_specs=pl.BlockSpec((CONV_TILE, CONV_CH), lambda bb, i: (bb * nt + i, 0)),
        out_shape=jax.ShapeDtypeStruct((t, CONV_CH), BF16),
        scratch_shapes=[pltpu.VMEM((CONV_HALO + CONV_TILE, CONV_CH), F32),
                        pltpu.VMEM((SUBLANES, CONV_HALO + CONV_ROWS, CONV_CH), F32)],
        compiler_params=_cparams(("arbitrary", "arbitrary")),
    )(proj, proj, proj, proj, w, b, ln_g, ln_b)


def _mlstm_kernel(q_ref, k_ref, v_ref, om_ref, gate_ref, cw_ref, cb_ref, gb_ref, o_ref,
                  ext, ct_s, n_s, m_s, *, batch):
    c = pl.program_id(0)
    lc, dh, nh = MLSTM_CHUNK, MLSTM_DH, MLSTM_HEADS
    assert lc == dh == LANES
    pad = SUBLANES

    @pl.when(c == 0)
    def _():
        ext[:, 0:pad, :] = jnp.zeros((batch, pad, 2 * MLSTM_W), F32)
        ct_s[...] = jnp.zeros_like(ct_s)
        n_s[...] = jnp.zeros_like(n_s)
        m_s[...] = jnp.zeros_like(m_s)

    @pl.when(c > 0)
    def _():
        ext[:, 0:pad, :] = ext[:, lc:lc + pad, :]

    ext[:, pad:, 0:MLSTM_W] = q_ref[...]
    ext[:, pad:, MLSTM_W:] = k_ref[...]

    row = lax.broadcasted_iota(I32, (lc, lc), 0)
    col = lax.broadcasted_iota(I32, (lc, lc), 1)
    causal = row >= col
    tril = causal.astype(F32)
    is_f = jnp.logical_and(col >= nh, col < 2 * nh)

    for b in range(batch):
        acc = jnp.broadcast_to(cb_ref[...], (lc, 2 * MLSTM_W))
        for k in range(MLSTM_CONV_K):
            lo = pad - (MLSTM_CONV_K - 1) + k
            acc = acc + cw_ref[k:k + 1, :] * ext[b, lo:lo + lc, :]
        qk = acc * _sigmoid(acc)

        gpre = gate_ref[b] + gb_ref[...]
        lf = jnp.where(is_f, _log_sigmoid(gpre), jnp.zeros_like(gpre))
        bcum = jnp.dot(tril, lf, precision=HIGHEST, preferred_element_type=F32)
        gpre_t = gpre.T
        bcum_t = bcum.T

        for h in range(nh):
            qh = qk[:, h * dh:(h + 1) * dh]
            kh = qk[:, MLSTM_W + h * dh:MLSTM_W + (h + 1) * dh] * (dh ** -0.5)
            vh = v_ref[b, :, h * dh:(h + 1) * dh]
            qb, kb, vb = qh.astype(BF16), kh.astype(BF16), vh.astype(BF16)
            bq = jnp.broadcast_to(bcum[:, nh + h:nh + h + 1], (lc, lc))
            br = bcum_t[nh + h:nh + h + 1, :]
            igr = gpre_t[h:h + 1, :]
            igc = jnp.broadcast_to(gpre[:, h:h + 1], (lc, lc))
            m_prev = m_s[b, h]
            ct = ct_s[b, h]
            n_row = n_s[b, h]

            dmat = jnp.where(causal, (bq - br) + igr, NEG)
            a_inter = bq + m_prev
            m_t = jnp.maximum(a_inter, jnp.max(dmat, axis=-1, keepdims=True))
            w_intra = jnp.exp(dmat - m_t)
            sc = lax.dot_general(qb, kb, (((1,), (1,)), ((), ())), preferred_element_type=F32) * w_intra
            s_inter = jnp.exp(a_inter - m_t)
            num = (jnp.dot(sc.astype(BF16), vb, preferred_element_type=F32)
                   + s_inter * jnp.dot(qb, ct.astype(BF16), preferred_element_type=F32))
            den = (jnp.sum(sc, axis=-1, keepdims=True)
                   + s_inter * jnp.sum(qh * n_row, axis=-1, keepdims=True))
            hout = num / jnp.maximum(jnp.abs(den), jnp.exp(-m_t))
            og = _sigmoid(om_ref[b, :, h * dh:(h + 1) * dh])
            o_ref[b, :, h * dh:(h + 1) * dh] = (og * hout).astype(o_ref.dtype)

            gtot = bq[lc - 1:lc, :]
            wst = (gtot - bq) + igc
            m_loc = jnp.max(wst, axis=0, keepdims=True)
            ek = jnp.exp(wst - m_loc) * kh
            c_loc = jnp.dot(ek.T.astype(BF16), vb, preferred_element_type=F32)
            n_loc = jnp.sum(ek, axis=0, keepdims=True)
            m_new = jnp.maximum(gtot + m_prev, m_loc)
            fa = jnp.exp(gtot + m_prev - m_new)
            fb = jnp.exp(m_loc - m_new)
            ct_s[b, h] = fa * ct + fb * c_loc
            n_s[b, h] = fa * n_row + fb * n_loc
            m_s[b, h] = m_new


def _mlstm(proj3, cw, cb, gb, batch, seq):
    nc = seq // MLSTM_CHUNK
    wide = lambda off: pl.BlockSpec((batch, MLSTM_CHUNK, MLSTM_W), lambda c, off=off: (0, c, off // MLSTM_W))
    full = lambda shape: pl.BlockSpec(shape, lambda c: (0,) * len(shape))
    return pl.pallas_call(
        functools.partial(_mlstm_kernel, batch=batch),
        grid=(nc,),
        in_specs=[wide(OFF_QM), wide(OFF_KM), wide(OFF_VM), wide(OFF_OM),
                  pl.BlockSpec((batch, MLSTM_CHUNK, LANES), lambda c: (0, c, OFF_IF // LANES)),
                  full((SUBLANES, 2 * MLSTM_W)), full((1, 2 * MLSTM_W)), full((1, LANES))],
        out_specs=pl.BlockSpec((batch, MLSTM_CHUNK, MLSTM_W), lambda c: (0, c, 0)),
        out_shape=jax.ShapeDtypeStruct((batch, seq, MLSTM_W), BF16),
        scratch_shapes=[pltpu.VMEM((batch, MLSTM_CHUNK + SUBLANES, 2 * MLSTM_W), F32),
                        pltpu.VMEM((batch, MLSTM_HEADS, MLSTM_DH, MLSTM_DH), F32),
                        pltpu.VMEM((batch, MLSTM_HEADS, 1, MLSTM_DH), F32),
                        pltpu.VMEM((batch, MLSTM_HEADS, 1, LANES), F32)],
        compiler_params=_cparams(("arbitrary",)),
    )(proj3, proj3, proj3, proj3, proj3, cw, cb, gb)


MERGE_ROWS = 256


def _merge_kernel(x_ref, gate_ref, ya_ref, yc_ref, ym_ref, wa_ref, wc_ref, wm_ref, wo_ref,
                  gf_ref, wr_ref, br_ref, xo_ref, h2_ref, lg_ref, *, tm):
    d = D_MODEL
    for r0 in range(0, tm, MERGE_ROWS):
        rows = slice(r0, r0 + MERGE_ROWS)
        branch = lambda y_ref, w_ref: jnp.dot(y_ref[rows, :], w_ref[...], preferred_element_type=F32)
        merged = (_sigmoid(gate_ref[rows, 0:d]) * branch(ya_ref, wa_ref)
                  + _sigmoid(gate_ref[rows, d:2 * d]) * branch(yc_ref, wc_ref)
                  + _sigmoid(gate_ref[rows, 2 * d:3 * d]) * branch(ym_ref, wm_ref))
        xn = x_ref[rows, :] + jnp.dot(merged.astype(BF16), wo_ref[...], preferred_element_type=F32)
        xo_ref[rows, :] = xn
        h2 = _rms(xn, gf_ref[...])
        lg_ref[rows, :] = jnp.dot(h2.astype(BF16), wr_ref[...], preferred_element_type=F32) + br_ref[...]
        for s in range(ROW_TILES):
            h2_ref[pl.ds(r0 * ROW_TILES + s, MERGE_ROWS, stride=ROW_TILES), :] = h2[:, s * LANES:(s + 1) * LANES]


def _merge(x2, proj, y_att, y_conv, y_m, wa, wc, wm, wo, g_ffn, w_r, b_r, tm=2 * MERGE_ROWS):
    t = x2.shape[0]
    rows = lambda n: pl.BlockSpec((tm, n), lambda i: (i, 0))
    full = lambda a: pl.BlockSpec(a.shape, lambda i: (0, 0))
    return pl.pallas_call(
        functools.partial(_merge_kernel, tm=tm),
        grid=(t // tm,),
        in_specs=[rows(D_MODEL), rows(3 * D_MODEL), rows(ATT_KV), rows(CONV_CH), rows(MLSTM_W),
                  full(wa), full(wc), full(wm), full(wo), full(g_ffn), full(w_r), full(b_r)],
        out_specs=[rows(D_MODEL), pl.BlockSpec((tm * ROW_TILES, LANES), lambda i: (i, 0)), rows(LANES)],
        out_shape=[jax.ShapeDtypeStruct((t, D_MODEL), F32),
                   jax.ShapeDtypeStruct((t * ROW_TILES, LANES), F32),
                   jax.ShapeDtypeStruct((t, LANES), F32)],
        compiler_params=_cparams(("arbitrary",)),
    )(x2, proj, y_att, y_conv, y_m, wa, wc, wm, wo, g_ffn, w_r, b_r)


ROUTER_TILE = 512


def _router_kernel(lg_ref, idx_ref, wt_ref, cnt_ref, run):
    n = ROUTER_TILE
    ng, ne = N_GROUPS_MOE, EXPERTS_PER_GROUP

    @pl.when(pl.program_id(0) == 0)
    def _():
        run[...] = jnp.zeros_like(run)

    lt = lg_ref[...].T
    lgrp = lt[0:ng, :]
    gmax = jnp.max(lgrp, axis=0, keepdims=True)
    grow = lax.broadcasted_iota(I32, (ng, n), 0)
    g_sel = jnp.min(jnp.where(lgrp == gmax, grow, ng), axis=0, keepdims=True)
    p_g = 1.0 / jnp.sum(jnp.exp(lgrp - gmax), axis=0, keepdims=True)

    le_sel = jnp.zeros((ne, n), F32)
    for g in range(ng):
        le_sel = jnp.where(g_sel == g, lt[(g + 1) * ne:(g + 2) * ne, :], le_sel)
    erow = lax.broadcasted_iota(I32, (ne, n), 0)
    v1 = jnp.max(le_sel, axis=0, keepdims=True)
    i1 = jnp.min(jnp.where(le_sel == v1, erow, ne), axis=0, keepdims=True)
    rest = jnp.where(erow == i1, -jnp.inf, le_sel)
    v2 = jnp.max(rest, axis=0, keepdims=True)
    i2 = jnp.min(jnp.where(rest == v2, erow, ne), axis=0, keepdims=True)
    e2 = jnp.exp(v2 - v1)
    w1 = p_g * (1.0 / (1.0 + e2))
    w2 = p_g * (e2 / (1.0 + e2))
    lo = jnp.minimum(i1, i2)
    hi = jnp.maximum(i1, i2)
    pair = ((lo * (2 * ne - 1 - lo)) >> 1) + (hi - lo - 1)
    cls = g_sel * PAIRS_PER_GROUP + pair
    first_is_lo = i1 < i2
    w_lo = jnp.where(first_is_lo, w1, w2)
    w_hi = jnp.where(first_is_lo, w2, w1)

    hit = lax.broadcasted_iota(I32, (LANES, n), 0) == cls
    onehot = jnp.where(hit, 1.0, 0.0)
    r_i = lax.broadcasted_iota(I32, (n, n), 0)
    c_i = lax.broadcasted_iota(I32, (n, n), 1)
    before = jnp.where(r_i < c_i, 1.0, 0.0).astype(BF16)
    cnt = jnp.dot(onehot.astype(BF16), before, preferred_element_type=F32) + run[:, 0:1]
    rank = jnp.sum(jnp.where(hit, cnt, 0.0), axis=0, keepdims=True)
    run[...] = run[...] + jnp.sum(onehot, axis=1, keepdims=True)

    idx_ref[...] = jnp.concatenate([cls, rank.astype(I32), jnp.zeros((SUBLANES - 2, n), I32)], axis=0)
    wt_ref[...] = jnp.concatenate([w_lo, w_hi, jnp.zeros((SUBLANES - 2, n), F32)], axis=0)
    cnt_ref[...] = run[...]


def _router(logits):
    t = logits.shape[0]
    n = ROUTER_TILE
    return pl.pallas_call(
        _router_kernel,
        grid=(t // n,),
        in_specs=[pl.BlockSpec((n, LANES), lambda i: (i, 0))],
        out_specs=[pl.BlockSpec((SUBLANES, n), lambda i: (0, i)),
                   pl.BlockSpec((SUBLANES, n), lambda i: (0, i)),
                   pl.BlockSpec((LANES, LANES), lambda i: (0, 0))],
        out_shape=[jax.ShapeDtypeStruct((SUBLANES, t), I32),
                   jax.ShapeDtypeStruct((SUBLANES, t), F32),
                   jax.ShapeDtypeStruct((LANES, LANES), F32)],
        scratch_shapes=[pltpu.VMEM((LANES, LANES), F32)],
        compiler_params=_cparams(("arbitrary",)),
    )(logits)


DISPATCH_TILE = 512


def _dispatch_kernel(dest_ref, x_ref, xs_in_ref, xs_ref, sem):
    del xs_in_ref
    n = DISPATCH_TILE

    def row_copy(t):
        d = pl.multiple_of(dest_ref[0, 0, t], ROW_TILES)
        src = x_ref.at[pl.ds(pl.multiple_of(t * ROW_TILES, ROW_TILES), ROW_TILES), :]
        return pltpu.make_async_copy(src, xs_ref.at[pl.ds(d, ROW_TILES), :], sem)

    def issue(g, carry):
        for j in range(DMA_GROUP):
            row_copy(g * DMA_GROUP + j).start(priority=j % 2)
        return carry

    lax.fori_loop(0, n // DMA_GROUP, issue, 0)

    def drain(t, carry):
        row_copy(t).wait()
        return carry

    lax.fori_loop(0, n, drain, 0, unroll=8)


def _dispatch(h2t, dest_tiles, p_len):
    t = h2t.shape[0] // ROW_TILES
    n = DISPATCH_TILE
    xs0 = jnp.zeros((p_len * ROW_TILES, LANES), F32)
    return pl.pallas_call(
        _dispatch_kernel,
        grid=(t // n,),
        in_specs=[pl.BlockSpec((1, 1, n), lambda i: (i, 0, 0), memory_space=pltpu.SMEM),
                  pl.BlockSpec((n * ROW_TILES, LANES), lambda i: (i, 0)),
                  pl.BlockSpec(memory_space=pl.ANY)],
        out_specs=pl.BlockSpec(memory_space=pl.ANY),
        out_shape=jax.ShapeDtypeStruct((p_len * ROW_TILES, LANES), F32),
        scratch_shapes=[pltpu.SemaphoreType.DMA(())],
        input_output_aliases={2: 0},
        compiler_params=_cparams(("arbitrary",)),
    )(dest_tiles, h2t, xs0)


def _expert_kernel(lo_ref, hi_ref, nu_ref, xs_ref, *refs):
    w_refs, y_ref = refs[:-1], refs[-1]
    first = pl.program_id(0) * EXPERT_BLOCKS
    used = jnp.clip(nu_ref[0] - first, 0, EXPERT_BLOCKS)

    def run(sub, nblk):
        n = nblk * MOE_BLOCK
        rows_in = sub * MOE_BLOCK * ROW_TILES
        rows_out = sub * MOE_BLOCK * PAIR_TILES
        xb = jnp.concatenate(
            [xs_ref[pl.ds(rows_in + s, n, stride=ROW_TILES), :] for s in range(ROW_TILES)], axis=-1).astype(BF16)
        for half in range(2):
            wg, wu, wd = w_refs[sub * 6 + half * 3:sub * 6 + half * 3 + 3]
            hg = jnp.dot(xb, wg[...], preferred_element_type=F32)
            hu = jnp.dot(xb, wu[...], preferred_element_type=F32)
            hb = (hg * _sigmoid(hg)) * hu
            y = jnp.dot(hb.astype(BF16), wd[...], preferred_element_type=F32)
            for s in range(ROW_TILES):
                y_ref[pl.ds(rows_out + half * ROW_TILES + s, n, stride=PAIR_TILES), :] = (
                    y[:, s * LANES:(s + 1) * LANES])

    def clear(sub):
        n = MOE_BLOCK * PAIR_TILES
        y_ref[sub * n:(sub + 1) * n, :] = jnp.zeros((n, LANES), F32)

    same = jnp.logical_and(lo_ref[first] == lo_ref[first + 1], hi_ref[first] == hi_ref[first + 1])
    both = used == EXPERT_BLOCKS

    @pl.when(jnp.logical_and(both, same))
    def _():
        run(0, 2)

    @pl.when(jnp.logical_and(both, jnp.logical_not(same)))
    def _():
        run(0, 1)
        run(1, 1)

    @pl.when(used == 1)
    def _():
        run(0, 1)
        clear(1)

    @pl.when(used == 0)
    def _():
        clear(0)
        clear(1)


def _experts(blk_lo, blk_hi, n_used, xs, wg, wu, wd):
    nb = blk_lo.shape[0]
    last = lambda b, nu: jnp.minimum(b, nu[0] - 1)
    xrows = pl.BlockSpec((EXPERT_BLOCKS * MOE_BLOCK * ROW_TILES, LANES), lambda i, lo, hi, nu: (i, 0))
    yrows = pl.BlockSpec((EXPERT_BLOCKS * MOE_BLOCK * PAIR_TILES, LANES), lambda i, lo, hi, nu: (i, 0))
    up, down = (D_MODEL, D_EXPERT), (D_EXPERT, D_MODEL)

    def weight(shape, sub, high):
        def index(i, lo, hi, nu):
            b = last(i * EXPERT_BLOCKS + sub, nu)
            return ((hi if high else lo)[b], 0, 0)
        return pl.BlockSpec((None,) + shape, index)

    w_specs, w_args = [], []
    for sub in range(EXPERT_BLOCKS):
        for high in (False, True):
            w_specs += [weight(up, sub, high), weight(up, sub, high), weight(down, sub, high)]
            w_args += [wg, wu, wd]
    return pl.pallas_call(
        _expert_kernel,
        grid_spec=pltpu.PrefetchScalarGridSpec(
            num_scalar_prefetch=3,
            grid=(nb // EXPERT_BLOCKS,),
            in_specs=[xrows] + w_specs,
            out_specs=yrows),
        out_shape=jax.ShapeDtypeStruct((xs.shape[0] * 2, LANES), F32),
        compiler_params=_cparams(("arbitrary",)),
    )(blk_lo, blk_hi, n_used, xs, *w_args)


COMBINE_TILE = 512
COMBINE_ROWS = 256


def _combine_kernel(dcur_ref, dnxt_ref, x_ref, wt_ref, p_ref, y_hbm, gp_ref, wpg_ref, wpp_ref, gfin_ref,
                    o_ref, ybuf, sems, *, final):
    n = COMBINE_TILE
    i = pl.program_id(0)
    slot = i % 2

    def row_copy(dref, sl, t):
        d = pl.multiple_of(dref[0, 0, t], PAIR_TILES)
        dst = ybuf.at[sl, pl.ds(pl.multiple_of(t * PAIR_TILES, PAIR_TILES), PAIR_TILES), :]
        return pltpu.make_async_copy(y_hbm.at[pl.ds(d, PAIR_TILES), :], dst, sems.at[sl])

    def issue(dref, sl):
        def body(g, carry):
            for j in range(DMA_GROUP):
                row_copy(dref, sl, g * DMA_GROUP + j).start(priority=j % 2)
            return carry
        lax.fori_loop(0, n // DMA_GROUP, body, 0)

    @pl.when(i == 0)
    def _():
        issue(dcur_ref, slot)

    @pl.when(i + 1 < pl.num_programs(0))
    def _():
        issue(dnxt_ref, 1 - slot)

    pes = [jnp.dot(p_ref[r0:r0 + COMBINE_ROWS, :].astype(BF16), wpp_ref[...], preferred_element_type=F32)
           for r0 in range(0, n, COMBINE_ROWS)]

    def drain(t, carry):
        row_copy(dcur_ref, slot, t).wait()
        return carry

    lax.fori_loop(0, n, drain, 0, unroll=8)

    def rows_of(half, r0):
        return jnp.concatenate(
            [ybuf[slot, pl.ds(r0 * PAIR_TILES + half * ROW_TILES + s, COMBINE_ROWS, stride=PAIR_TILES), :]
             for s in range(ROW_TILES)], axis=-1)

    for part, r0 in enumerate(range(0, n, COMBINE_ROWS)):
        rows = slice(r0, r0 + COMBINE_ROWS)
        xn = x_ref[rows, :] + (wt_ref[rows, 0:1] * rows_of(0, r0) + wt_ref[rows, 1:2] * rows_of(1, r0))
        hp = _rms(xn, gp_ref[...])
        gate = _sigmoid(jnp.dot(hp.astype(BF16), wpg_ref[...], preferred_element_type=F32))
        xo = xn + gate * pes[part]
        if final:
            xo = _rms(xo, gfin_ref[...])
        o_ref[rows, :] = xo


def _combine(dest_tiles, x2, wts, p2, y, g_ple, wpg, wpp, g_final, final):
    t = x2.shape[0]
    n = COMBINE_TILE
    nt = t // n
    rows = lambda w: pl.BlockSpec((n, w), lambda i: (i, 0))
    full = lambda a: pl.BlockSpec(a.shape, lambda i: (0, 0))
    return pl.pallas_call(
        functools.partial(_combine_kernel, final=final),
        grid=(nt,),
        in_specs=[pl.BlockSpec((1, 1, n), lambda i: (i, 0, 0), memory_space=pltpu.SMEM),
                  pl.BlockSpec((1, 1, n), lambda i: (jnp.minimum(i + 1, nt - 1), 0, 0), memory_space=pltpu.SMEM),
                  rows(D_MODEL), rows(2), rows(PLE_DIM),
                  pl.BlockSpec(memory_space=pl.ANY),
                  full(g_ple), full(wpg), full(wpp), full(g_final)],
        out_specs=rows(D_MODEL),
        out_shape=jax.ShapeDtypeStruct((t, D_MODEL), F32),
        scratch_shapes=[pltpu.VMEM((2, n * PAIR_TILES, LANES), F32), pltpu.SemaphoreType.DMA((2,))],
        compiler_params=_cparams(("arbitrary",)),
    )(dest_tiles, dest_tiles, x2, wts, p2, y, g_ple, wpg, wpp, g_final)


CAST_BLOCK_BYTES = 2 * 1024 * 1024


def _cast_kernel(w_ref, o_ref):
    o_ref[...] = w_ref[...].astype(o_ref.dtype)


def _layer_bf16(w, layer):
    cols = w.shape[-1]
    rows = int(np.prod(w.shape[1:-1]))
    rb = min(rows, CAST_BLOCK_BYTES // (4 * cols))
    out = pl.pallas_call(
        _cast_kernel,
        grid=(rows // rb,),
        in_specs=[pl.BlockSpec((None, rb, cols), lambda i: (layer, i, 0))],
        out_specs=pl.BlockSpec((rb, cols), lambda i: (i, 0)),
        out_shape=jax.ShapeDtypeStruct((rows, cols), BF16),
        compiler_params=_cparams(("arbitrary",)),
    )(w.reshape(w.shape[0], rows, cols))
    return out.reshape(w.shape[1:])


def _class_experts():
    lo, hi = [], []
    for g in range(N_GROUPS_MOE):
        for a in range(EXPERTS_PER_GROUP):
            for b in range(a + 1, EXPERTS_PER_GROUP):
                lo.append(g * EXPERTS_PER_GROUP + a)
                hi.append(g * EXPERTS_PER_GROUP + b)
    return np.asarray(lo, np.int32), np.asarray(hi, np.int32)


def _moe_plan(idx, cnt, t):
    counts = cnt[:N_CLASSES, 0].astype(I32)
    padded = (counts + MOE_BLOCK - 1) // MOE_BLOCK * MOE_BLOCK
    ends_p = jnp.cumsum(padded)
    pstart = ends_p - padded
    classes = jnp.arange(N_CLASSES, dtype=I32)
    dest = jnp.sum(jnp.where(idx[0][:, None] == classes, pstart, 0), axis=-1) + idx[1]
    p_len = t + N_CLASSES * MOE_BLOCK
    nb = p_len // MOE_BLOCK
    blk_start = jnp.arange(nb, dtype=I32) * MOE_BLOCK
    blk_cls = jnp.minimum(jnp.sum((ends_p[None, :] <= blk_start[:, None]).astype(I32), axis=1), N_CLASSES - 1)
    lo_tab, hi_tab = _class_experts()
    is_c = blk_cls[:, None] == classes
    blk_lo = jnp.sum(jnp.where(is_c, lo_tab, 0), axis=1).astype(I32)
    blk_hi = jnp.sum(jnp.where(is_c, hi_tab, 0), axis=1).astype(I32)
    n_used = (ends_p[-1:] // MOE_BLOCK).astype(I32)
    return dest, blk_lo, blk_hi, n_used, p_len


def kernel(x, p, g_mix, w_in, conv_w, conv_b, conv_ln_g, conv_ln_b, mlstm_conv_w, mlstm_conv_b,
           mlstm_i_bias, mlstm_f_bias, w_br_att, w_br_conv, w_br_mlstm, w_out, g_ffn,
           w_router_g, b_router_g, w_router_e, b_router_e, w_exp_gate, w_exp_up, w_exp_down,
           g_ple, w_ple_gate, w_ple_proj, g_final):
    batch, seq, d = x.shape
    depth = w_in.shape[0]
    t = batch * seq
    bias_tab = jnp.asarray(_att_bias_table())
    x2 = x.reshape(t, d)
    nr = N_GROUPS_MOE + N_EXPERTS
    for i in range(depth):
        w = w_in[i]
        wp = jnp.concatenate([w[:, ORIG_SMALL + 2 * MLSTM_HEADS:], w[:, :ORIG_SMALL],
                              w[:, ORIG_SMALL:ORIG_SMALL + 2 * MLSTM_HEADS],
                              jnp.zeros((d, LANES - 2 * MLSTM_HEADS), F32)], axis=1).astype(BF16)
        proj = _inproj(x2, g_mix[i][None, :], wp)

        y_att = _attention(proj, bias_tab, batch, seq)
        cw = jnp.concatenate([conv_w[i], jnp.zeros((CONV_HALO - CONV_K, CONV_CH), F32)], axis=0)
        y_conv = _conv_module(proj, cw, conv_b[i][None, :], conv_ln_g[i][None, :], conv_ln_b[i][None, :],
                              batch, seq)
        mw = jnp.concatenate([mlstm_conv_w[i], jnp.zeros((SUBLANES - MLSTM_CONV_K, 2 * MLSTM_W), F32)], axis=0)
        gb = jnp.concatenate([mlstm_i_bias[i], mlstm_f_bias[i],
                              jnp.zeros((LANES - 2 * MLSTM_HEADS,), F32)])[None, :]
        y_m = _mlstm(proj.reshape(batch, seq, PROJ_N), mw, mlstm_conv_b[i][None, :], gb, batch, seq)

        gpad = EXPERTS_PER_GROUP - N_GROUPS_MOE
        w_r = jnp.concatenate([w_router_g[i], jnp.zeros((d, gpad), F32), w_router_e[i],
                               jnp.zeros((d, LANES - nr - gpad), F32)], axis=1)
        b_r = jnp.concatenate([b_router_g[i], jnp.zeros((gpad,), F32), b_router_e[i],
                               jnp.zeros((LANES - nr - gpad,), F32)])[None, :]
        x2, h2t, logits = _merge(x2, proj, y_att, y_conv, y_m.reshape(t, MLSTM_W),
                                 w_br_att[i].astype(BF16), w_br_conv[i].astype(BF16),
                                 w_br_mlstm[i].astype(BF16), w_out[i].astype(BF16),
                                 g_ffn[i][None, :], w_r.astype(BF16), b_r)

        idx, wt, cnt = _router(logits)
        dest, blk_lo, blk_hi, n_used, p_len = _moe_plan(idx, cnt, t)
        xs = _dispatch(h2t, (dest * ROW_TILES).reshape(t // DISPATCH_TILE, 1, DISPATCH_TILE), p_len)
        ys = _experts(blk_lo, blk_hi, n_used, xs, _layer_bf16(w_exp_gate, i), _layer_bf16(w_exp_up, i),
                      _layer_bf16(w_exp_down, i))
        x2 = _combine((dest * PAIR_TILES).reshape(t // COMBINE_TILE, 1, COMBINE_TILE), x2, wt[0:2].T,
                      p[i].reshape(t, PLE_DIM), ys, g_ple[i][None, :], w_ple_gate[i].astype(BF16),
                      w_ple_proj[i].astype(BF16), g_final[None, :], final=(i == depth - 1))
    return x2.reshape(batch, seq, d)
```
